```python
import math
import jax, jax.numpy as jnp
from jax import lax
import numpy as np

D_MODEL = 1024
BATCH = 2
SEQ = 8192
DEPTH = 2

N_MIXERS = 2
N_HEADS = 16
HEAD_DIM = 64
WIDTH = N_HEADS * HEAD_DIM
MOBA_BLOCK = 256
MOBA_TOPK = 3
Q_CHUNK = 64
DECAY_LORA = 64
AAA_LORA = 64
N_SHIFT_STREAMS = 6
NORM_EPS = 1e-6
LNX_EPS = 64e-5

kernel_name = "hybrid_moba_rwkv7_trunk"


def rms_norm(x, g):
    xf = x.astype(jnp.float32)
    y = xf * lax.rsqrt(jnp.mean(xf * xf, axis=-1, keepdims=True) + NORM_EPS)
    return (y * g.astype(jnp.float32)).astype(x.dtype)


def alibi_slopes(n):
    return jnp.asarray([2.0 ** (-8.0 * (i + 1) / n) for i in range(n)], dtype=jnp.float32)


def to_heads(t):
    b, s, _ = t.shape
    return t.reshape(b, s, N_HEADS, HEAD_DIM).transpose(0, 2, 1, 3)


def moba_attention(q, k, v):
    B, H, S, dh = q.shape
    nb = -(-S // MOBA_BLOCK)
    s_pad = nb * MOBA_BLOCK
    pad = ((0, 0), (0, 0), (0, s_pad - S), (0, 0))
    kp = jnp.pad(k, pad)
    vp = jnp.pad(v, pad)
    k_blocks = kp.reshape(B, H, nb, MOBA_BLOCK, dh)
    v_blocks = vp.reshape(B, H, nb, MOBA_BLOCK, dh)
    k_mean = jnp.mean(k_blocks.astype(jnp.float32), axis=3).astype(q.dtype)
    topk = min(MOBA_TOPK, nb)
    slopes = alibi_slopes(H)[None, :, None, None]
    scale = dh ** -0.5
    b_idx = jnp.arange(B)[:, None, None, None]
    h_idx = jnp.arange(H)[None, :, None, None]
    blk_pos = jnp.arange(MOBA_BLOCK)

    def chunk(c):
        t0 = c * Q_CHUNK
        own = t0 // MOBA_BLOCK
        q_c = lax.dynamic_slice_in_dim(q, t0, Q_CHUNK, axis=2)
        t_int = t0 + jnp.arange(Q_CHUNK)
        t_f = t_int.astype(jnp.float32)
        gate = jnp.einsum('bhqd,bhnd->bhqn', q_c, k_mean).astype(jnp.float32)
        gate = jnp.where(jnp.arange(nb) < own, gate, -jnp.inf)
        _, sel = lax.top_k(gate, topk)
        sel_valid = jnp.arange(topk) < own
        k_sel = k_blocks[b_idx, h_idx, sel]
        v_sel = v_blocks[b_idx, h_idx, sel]
        s_sel = jnp.einsum('bhqd,bhqjld->bhqjl', q_c, k_sel).astype(jnp.float32) * scale
        pos_sel = (sel[..., None] * MOBA_BLOCK + blk_pos).astype(jnp.float32)
        s_sel = s_sel - slopes[..., None] * (t_f[:, None, None] - pos_sel)
        s_sel = jnp.where(sel_valid[:, None], s_sel, -jnp.inf)
        s_sel = s_sel.reshape(B, H, Q_CHUNK, topk * MOBA_BLOCK)
        k_own = lax.dynamic_slice_in_dim(kp, own * MOBA_BLOCK, MOBA_BLOCK, axis=2)
        v_own = lax.dynamic_slice_in_dim(vp, own * MOBA_BLOCK, MOBA_BLOCK, axis=2)
        pos_own = own * MOBA_BLOCK + blk_pos
        s_own = jnp.einsum('bhqd,bhld->bhql', q_c, k_own).astype(jnp.float32) * scale
        s_own = s_own - slopes * (t_f[:, None] - pos_own.astype(jnp.float32)[None, :])
        s_own = jnp.where(pos_own[None, :] <= t_int[:, None], s_own, -jnp.inf)
        p = jax.nn.softmax(jnp.concatenate([s_sel, s_own], axis=-1), axis=-1)
        p_sel = p[..., :topk * MOBA_BLOCK].reshape(B, H, Q_CHUNK, topk, MOBA_BLOCK).astype(v.dtype)
        p_own = p[..., topk * MOBA_BLOCK:].astype(v.dtype)
        return (jnp.einsum('bhqjl,bhqjld->bhqd', p_sel, v_sel)
                + jnp.einsum('bhql,bhld->bhqd', p_own, v_own))

    out = lax.map(chunk, jnp.arange(S // Q_CHUNK))
    return out.transpose(1, 2, 0, 3, 4).reshape(B, H, S, dh)


def moba_layer(x, norm_g, w_in, w_out):
    B, S, _ = x.shape
    h = rms_norm(x, norm_g)
    proj = h @ w_in
    q, k, v, gate = jnp.split(proj, 4, axis=-1)
    o = moba_attention(to_heads(q), to_heads(k), to_heads(v))
    o = o.transpose(0, 2, 1, 3).reshape(B, S, WIDTH)
    return x + (o * jax.nn.silu(gate)) @ w_out


def rwkv7_scan(r, w, k, v, kk, a):
    B, S, H, dh = r.shape
    seq_first = lambda t: jnp.moveaxis(t, 1, 0)

    def step(state, inp):
        r_t, w_t, k_t, v_t, kk_t, a_t = inp
        sa = jnp.einsum('bhvk,bhk->bhv', state, -kk_t)
        state = (state * w_t[:, :, None, :]
                 + sa[..., None] * (kk_t * a_t)[:, :, None, :]
                 + v_t[..., None] * k_t[:, :, None, :])
        return state, jnp.einsum('bhvk,bhk->bhv', state, r_t)

    s0 = jnp.zeros((B, H, dh, dh), jnp.float32)
    _, y = lax.scan(step, s0, tuple(seq_first(t) for t in (r, w, k, v, kk, a)))
    return jnp.moveaxis(y, 0, 1)


def rwkv7_layer(x, norm_g, mix, w_in, w0, w1, w2, a0, a1, a2, k_k, k_a, r_k, lnx_g, lnx_b, w_out):
    B, S, _ = x.shape
    f32 = jnp.float32
    h = rms_norm(x, norm_g)
    h_prev = jnp.pad(h, ((0, 0), (1, 0), (0, 0)))[:, :-1]
    xx = h_prev - h
    xs = h[:, :, None, :] + xx[:, :, None, :] * mix
    proj = jnp.einsum('bsnd,ndw->bsnw', xs[:, :, :4], w_in)
    r, k, v, g = proj[:, :, 0], proj[:, :, 1], proj[:, :, 2], proj[:, :, 3]
    xw, xa = xs[:, :, 4], xs[:, :, 5]
    w_log = -jax.nn.softplus(-(w0 + jnp.tanh(xw @ w1) @ w2).astype(f32)) - 0.5
    decay = jnp.exp(-jnp.exp(w_log))
    a = jax.nn.sigmoid((a0 + (xa @ a1) @ a2).astype(f32))
    hd = lambda t: t.astype(f32).reshape(B, S, N_HEADS, HEAD_DIM)
    hp = lambda p: p.astype(f32).reshape(N_HEADS, HEAD_DIM)
    kk = hd(k * k_k)
    kk = kk / jnp.maximum(jnp.sqrt(jnp.sum(kk * kk, axis=-1, keepdims=True)), 1e-12)
    k_mod = hd(k) * (1.0 + (hd(a) - 1.0) * hp(k_a))
    r_h, v_h, a_h = hd(r), hd(v), hd(a)
    y = rwkv7_scan(r_h, hd(decay), k_mod, v_h, kk, a_h)
    mu = jnp.mean(y, axis=-1, keepdims=True)
    var = jnp.mean(jnp.square(y - mu), axis=-1, keepdims=True)
    y = (y - mu) * lax.rsqrt(var + LNX_EPS)
    y = y * hp(lnx_g) + hp(lnx_b)
    bonus = jnp.sum(r_h * k_mod * r_k.astype(f32), axis=-1, keepdims=True) * v_h
    o = (y + bonus).reshape(B, S, WIDTH).astype(x.dtype)
    return x + (o * jax.nn.silu(g)) @ w_out


def setup_inputs(seed: int = 0) -> dict:
    key = jax.random.key(seed)
    ks = jax.random.split(key, 24)
    nrm = lambda k, shape, s: jax.random.normal(k, shape, jnp.float32) * s
    D, W = D_MODEL, WIDTH
    return {
        "x": nrm(ks[0], (BATCH, SEQ, D), 1.0),
        "moba_norm_g": 1.0 + nrm(ks[1], (D,), 0.05),
        "moba_w_in": nrm(ks[2], (D, 4 * W), D ** -0.5),
        "moba_w_out": nrm(ks[3], (W, D), W ** -0.5),
        "rwkv_norm_g": 1.0 + nrm(ks[4], (D,), 0.05),
        "rwkv_mix": jax.random.uniform(ks[5], (N_SHIFT_STREAMS, D), jnp.float32),
        "rwkv_w_in": nrm(ks[6], (4, D, W), D ** -0.5),
        "rwkv_w0": jax.random.uniform(ks[7], (W,), jnp.float32, -4.0, 1.0),
        "rwkv_w1": nrm(ks[8], (D, DECAY_LORA), D ** -0.5),
        "rwkv_w2": nrm(ks[9], (DECAY_LORA, W), 0.3 * DECAY_LORA ** -0.5),
        "rwkv_a0": nrm(ks[10], (W,), 0.1),
        "rwkv_a1": nrm(ks[11], (D, AAA_LORA), D ** -0.5),
        "rwkv_a2": nrm(ks[12], (AAA_LORA, W), 0.3 * AAA_LORA ** -0.5),
        "rwkv_k_k": 0.85 + nrm(ks[13], (W,), 0.05),
        "rwkv_k_a": 1.0 + nrm(ks[14], (W,), 0.05),
        "rwkv_r_k": nrm(ks[15], (N_HEADS, HEAD_DIM), 0.1),
        "rwkv_lnx_g": 1.0 + nrm(ks[16], (W,), 0.05),
        "rwkv_lnx_b": nrm(ks[17], (W,), 0.02),
        "rwkv_w_out": nrm(ks[18], (W, D), W ** -0.5),
        "final_norm_g": 1.0 + nrm(ks[19], (D,), 0.05),
    }


def reference(x, moba_norm_g, moba_w_in, moba_w_out, rwkv_norm_g, rwkv_mix, rwkv_w_in,
              rwkv_w0, rwkv_w1, rwkv_w2, rwkv_a0, rwkv_a1, rwkv_a2, rwkv_k_k, rwkv_k_a,
              rwkv_r_k, rwkv_lnx_g, rwkv_lnx_b, rwkv_w_out, final_norm_g):
    moba_params = (moba_norm_g, moba_w_in, moba_w_out)
    rwkv_params = (rwkv_norm_g, rwkv_mix, rwkv_w_in, rwkv_w0, rwkv_w1, rwkv_w2, rwkv_a0,
                   rwkv_a1, rwkv_a2, rwkv_k_k, rwkv_k_a, rwkv_r_k, rwkv_lnx_g, rwkv_lnx_b,
                   rwkv_w_out)
    for i in range(DEPTH):
        if i % N_MIXERS == 0:
            x = moba_layer(x, *moba_params)
        else:
            x = rwkv7_layer(x, *rwkv_params)
    return rms_norm(x, final_norm_g)
```

```python
import functools

import jax
import jax.numpy as jnp
from jax import lax
from jax.experimental import pallas as pl
from jax.experimental.pallas import tpu as pltpu

F32 = jnp.float32
BF16 = jnp.bfloat16

D_MODEL = 1024
N_HEADS = 16
HEAD_DIM = 64
WIDTH = N_HEADS * HEAD_DIM
LANES = 128
N_PAIRS = WIDTH // LANES
MOBA_BLOCK = 256
MOBA_TOPK = 3
LORA = 64
NORM_EPS = 1e-6
LNX_EPS = 64e-5
CHUNK = 64
VMEM_LIMIT = 56 * 1024 * 1024
NEG_INF = float("-inf")


def _dot(a, b):
    return jnp.dot(a, b, preferred_element_type=F32)


def _dot_nt(a, b):
    return lax.dot_general(a, b, (((1,), (1,)), ((), ())), preferred_element_type=F32)


def _dot_tn(a, b):
    return lax.dot_general(a, b, (((0,), (0,)), ((), ())), preferred_element_type=F32)


def _split2(x):
    hi = x.astype(BF16)
    lo = (x - hi.astype(F32)).astype(BF16)
    return hi, lo


def _split3(x):
    hi = x.astype(BF16)
    r1 = x - hi.astype(F32)
    mid = r1.astype(BF16)
    lo = (r1 - mid.astype(F32)).astype(BF16)
    return hi, mid, lo


def _mm3(a, b, dot=_dot):
    ah, al = _split2(a)
    bh, bl = _split2(b)
    return dot(ah, bh) + dot(ah, bl) + dot(al, bh)


def _mm2(a, b_exact, dot=_dot):
    ah, al = _split2(a)
    return dot(ah, b_exact) + dot(al, b_exact)


def _rms(x, g):
    ms = jnp.mean(x * x, axis=-1, keepdims=True)
    return x * lax.rsqrt(ms + NORM_EPS) * g


def _iota(shape, dim):
    return lax.broadcasted_iota(jnp.int32, shape, dim)


def _head_blockdiag(n):
    return ((_iota((n, n), 0) // HEAD_DIM) == (_iota((n, n), 1) // HEAD_DIM)).astype(BF16)


def _const_spec(shape):
    nd = len(shape)
    return pl.BlockSpec(shape, lambda *_: (0,) * nd)


def _params(n_axes):
    return pltpu.CompilerParams(dimension_semantics=("arbitrary",) * n_axes,
                                vmem_limit_bytes=VMEM_LIMIT)


def _moba_proj_kernel(x_ref, g_ref, whi_ref, wlo_ref, q_ref, k_ref, v_ref, sg_ref):
    h = _rms(x_ref[...], g_ref[...])
    hh, hl = _split2(h)

    def proj3(slab):
        cols = slice(slab * WIDTH, (slab + 1) * WIDTH)
        return (_dot(hh, whi_ref[:, cols]) + _dot(hh, wlo_ref[:, cols])
                + _dot(hl, whi_ref[:, cols]))

    q = proj3(0)
    k = proj3(1)
    v = _dot(hh, whi_ref[:, 2 * WIDTH:3 * WIDTH])
    gate = _dot(hh, whi_ref[:, 3 * WIDTH:4 * WIDTH])
    for p in range(N_PAIRS):
        cols = slice(p * LANES, (p + 1) * LANES)
        q_ref[0, p] = q[:, cols]
        k_ref[0, p] = k[:, cols].astype(BF16)
        v_ref[0, p] = v[:, cols].astype(BF16)
    sg_ref[...] = (gate * jax.nn.sigmoid(gate)).astype(BF16)


def _moba_proj(x2d, norm_g, w_hi, w_lo, batch, seq, tm):
    m = x2d.shape[0]
    nsb = seq // tm
    pair_spec = pl.BlockSpec((1, N_PAIRS, tm, LANES), lambda i: (i // nsb, 0, i % nsb, 0))
    pair_shape = (batch, N_PAIRS, seq, LANES)
    return pl.pallas_call(
        _moba_proj_kernel,
        grid=(m // tm,),
        in_specs=[
            pl.BlockSpec((tm, D_MODEL), lambda i: (i, 0)),
            _const_spec((1, D_MODEL)),
            _const_spec((D_MODEL, 4 * WIDTH)),
            _const_spec((D_MODEL, 2 * WIDTH)),
        ],
        out_specs=[pair_spec, pair_spec, pair_spec,
                   pl.BlockSpec((tm, WIDTH), lambda i: (i, 0))],
        out_shape=[
            jax.ShapeDtypeStruct(pair_shape, F32),
            jax.ShapeDtypeStruct(pair_shape, BF16),
            jax.ShapeDtypeStruct(pair_shape, BF16),
            jax.ShapeDtypeStruct((m, WIDTH), BF16),
        ],
        compiler_params=_params(1),
        name="moba_proj",
    )(x2d, norm_g, w_hi, w_lo)


def _moba_attn_kernel(slopes_ref, q_ref, k_ref, v_ref, o_ref, kmean_ref, bias_ref, *, nb):
    L = MOBA_BLOCK
    p = pl.program_id(1)
    n = pl.program_id(2)

    @pl.when(n == 0)
    def _():
        for j in range(nb):
            kb = k_ref[0, 0, j * L:(j + 1) * L, :].astype(F32)
            kmean_ref[j:j + 1, :] = jnp.mean(kb, axis=0, keepdims=True)
        rel = (_iota((L, L), 1) - _iota((L, L), 0)).astype(F32)
        for hh in range(2):
            bias_ref[hh] = rel * slopes_ref[2 * p + hh]

    qf = q_ref[0, 0]
    lane = _iota((L, LANES), 1)
    head_lo = lane < HEAD_DIM
    head_masks = (head_lo, jnp.logical_not(head_lo))
    blk = _iota((L, nb), 1)
    kmean = kmean_ref[...]
    row = _iota((L, L), 0)
    col = _iota((L, L), 1)
    start = pl.multiple_of(n * L, L)
    k_own = k_ref[0, 0, pl.ds(start, L), :]
    v_own = v_ref[0, 0, pl.ds(start, L), :]

    q_heads, sel_heads, stats = [], [], []
    acc = None
    for hh in range(2):
        qh = jnp.where(head_masks[hh], qf, 0.0)
        gate = _mm3(qh, kmean, _dot_nt)
        gate = jnp.where(blk < n, gate, NEG_INF)
        sel = jnp.zeros((L, nb), F32)
        for _ in range(MOBA_TOPK):
            mx = jnp.max(gate, axis=1, keepdims=True)
            first = jnp.min(jnp.where(gate == mx, blk, nb), axis=1, keepdims=True)
            pick = jnp.logical_and(blk == first, mx > NEG_INF)
            sel = jnp.where(pick, 1.0, sel)
            gate = jnp.where(pick, NEG_INF, gate)
        qb = (qh * (HEAD_DIM ** -0.5)).astype(BF16)
        s = _dot_nt(qb, k_own) + bias_ref[hh]
        s = jnp.where(col <= row, s, NEG_INF)
        m = jnp.max(s, axis=1, keepdims=True)
        pr = jnp.exp(s - m)
        l = jnp.sum(pr, axis=1, keepdims=True)
        pv = _dot(pr.astype(BF16), v_own)
        acc = pv if hh == 0 else jnp.where(head_lo, acc, pv)
        q_heads.append(qb)
        sel_heads.append(sel)
        stats += [m, l]

    def body(j, carry):
        m0, l0, m1, l1, acc = carry
        off = pl.multiple_of(j * L, L)
        kj = k_ref[0, 0, pl.ds(off, L), :]
        vj = v_ref[0, 0, pl.ds(off, L), :]
        dist = ((j - n) * L).astype(F32)
        new, pvs, alphas = [], [], []
        for hh, (m, l) in enumerate(((m0, l0), (m1, l1))):
            chosen = jnp.sum(jnp.where(blk == j, sel_heads[hh], 0.0), axis=1, keepdims=True) > 0.0
            s = _dot_nt(q_heads[hh], kj) + (bias_ref[hh] + slopes_ref[2 * p + hh] * dist)
            s = jnp.where(chosen, s, NEG_INF)
            m_new = jnp.maximum(m, jnp.max(s, axis=1, keepdims=True))
            alpha = jnp.exp(m - m_new)
            pr = jnp.exp(s - m_new)
            l_new = alpha * l + jnp.sum(pr, axis=1, keepdims=True)
            pvs.append(_dot(pr.astype(BF16), vj))
            alphas.append(alpha)
            new += [m_new, l_new]
        acc = (acc * jnp.where(head_lo, alphas[0], alphas[1])
               + jnp.where(head_lo, pvs[0], pvs[1]))
        return (*new, acc)

    m0, l0, m1, l1, acc = lax.fori_loop(0, n, body, (*stats, acc))
    o_ref[0, 0] = (acc / jnp.where(head_lo, l0, l1)).astype(o_ref.dtype)


def _moba_attn(slopes, q, k, v):
    batch, _, seq, _ = q.shape
    nb = seq // MOBA_BLOCK
    tile = pl.BlockSpec((1, 1, MOBA_BLOCK, LANES), lambda b, p, n: (b, p, n, 0))
    full = pl.BlockSpec((1, 1, seq, LANES), lambda b, p, n: (b, p, 0, 0))
    return pl.pallas_call(
        functools.partial(_moba_attn_kernel, nb=nb),
        grid=(batch, N_PAIRS, nb),
        in_specs=[pl.BlockSpec(memory_space=pltpu.SMEM), tile, full, full],
        out_specs=tile,
        out_shape=jax.ShapeDtypeStruct(q.shape, BF16),
        scratch_shapes=[pltpu.VMEM((nb, LANES), F32),
                        pltpu.VMEM((2, MOBA_BLOCK, MOBA_BLOCK), F32)],
        compiler_params=_params(3),
        name="moba_attn",
    )(slopes, q, k, v)


def _out_proj_kernel(o_ref, sg_ref, x_ref, w_ref, g_ref, y_ref, *, final_norm):
    o = jnp.concatenate([o_ref[0, p] for p in range(N_PAIRS)], axis=1)
    gated = (o.astype(F32) * sg_ref[...].astype(F32)).astype(BF16)
    y = x_ref[...] + _dot(gated, w_ref[...])
    if final_norm:
        y = _rms(y, g_ref[...])
    y_ref[...] = y


def _out_proj(o_pairs, sg, x2d, w_bf16, norm_g, seq, tm, final_norm):
    m = x2d.shape[0]
    nsb = seq // tm
    row_spec = pl.BlockSpec((tm, D_MODEL), lambda i: (i, 0))
    return pl.pallas_call(
        functools.partial(_out_proj_kernel, final_norm=final_norm),
        grid=(m // tm,),
        in_specs=[
            pl.BlockSpec((1, N_PAIRS, tm, LANES), lambda i: (i // nsb, 0, i % nsb, 0)),
            row_spec, row_spec,
            _const_spec((WIDTH, D_MODEL)),
            _const_spec((1, D_MODEL)),
        ],
        out_specs=row_spec,
        out_shape=jax.ShapeDtypeStruct((m, D_MODEL), F32),
        compiler_params=_params(1),
        name="out_proj_final" if final_norm else "out_proj",
    )(o_pairs, sg, x2d, w_bf16, norm_g)


def _rwkv_proj_kernel(x_ref, halo_ref, g_ref, mix_ref, win_ref, w0_ref, w1h_ref, w1l_ref,
                      w2h_ref, w2l_ref, a0_ref, a1h_ref, a1l_ref, a2h_ref, a2l_ref,
                      kk_ref, ka_ref, rk_ref,
                      r_out, k_out, v_out, lw_out, kk_out, bb_out, bonus_out, sg_out,
                      *, tiles_per_seq):
    i = pl.program_id(0)
    tm = x_ref.shape[0]
    g = g_ref[...]
    h = _rms(x_ref[...], g)
    prev_row = _rms(halo_ref[...], g)[7:8, :]
    prev_row = jnp.where(i % tiles_per_seq == 0, 0.0, prev_row)
    rolled = pltpu.roll(h, 1, 0)
    h_prev = jnp.where(_iota((tm, D_MODEL), 0) == 0, prev_row, rolled)
    xx = h_prev - h

    def stream(n):
        return h + xx * mix_ref[n:n + 1, :]

    r = _dot(stream(0).astype(BF16), win_ref[0])
    k = _dot(stream(1).astype(BF16), win_ref[1])
    v = _dot(stream(2).astype(BF16), win_ref[2])
    gt = _dot(stream(3).astype(BF16), win_ref[3])
    sg_out[...] = (gt * jax.nn.sigmoid(gt)).astype(BF16)

    def lora(xs, ah_ref, al_ref, bh_ref, bl_ref, act):
        xh, xl = _split2(xs)
        mid = _dot(xh, ah_ref[...]) + _dot(xh, al_ref[...]) + _dot(xl, ah_ref[...])
        mid = act(mid)
        mh, ml = _split2(mid)
        return _dot(mh, bh_ref[...]) + _dot(mh, bl_ref[...]) + _dot(ml, bh_ref[...])

    z = -(w0_ref[...] + lora(stream(4), w1h_ref, w1l_ref, w2h_ref, w2l_ref, jnp.tanh))
    softplus = jnp.maximum(z, 0.0) + jnp.log(1.0 + jnp.exp(-jnp.abs(z)))
    lw = -jnp.exp(-softplus - 0.5)
    a = jax.nn.sigmoid(a0_ref[...] + lora(stream(5), a1h_ref, a1l_ref, a2h_ref, a2l_ref,
                                          lambda t: t))
    kr = k * kk_ref[...]
    k_mod = k * (1.0 + (a - 1.0) * ka_ref[...])
    rk = r * k_mod * rk_ref[...]
    ones_bd = _head_blockdiag(LANES)
    for p in range(N_PAIRS):
        cols = slice(p * LANES, (p + 1) * LANES)
        kr_p = kr[:, cols]
        ss = _mm2(kr_p * kr_p, ones_bd)
        kk_p = kr_p / jnp.maximum(jnp.sqrt(ss), 1e-12)
        r_out[0, p] = r[:, cols]
        k_out[0, p] = k_mod[:, cols]
        v_out[0, p] = v[:, cols]
        lw_out[0, p] = lw[:, cols]
        kk_out[0, p] = kk_p
        bb_out[0, p] = kk_p * a[:, cols]
        bonus_out[0, p] = _mm2(rk[:, cols], ones_bd) * v[:, cols]


def _rwkv_proj(x2d, weights, batch, seq, tm):
    m = x2d.shape[0]
    nsb = seq // tm
    halo_blocks = tm // 8
    pair_spec = pl.BlockSpec((1, N_PAIRS, tm, LANES), lambda i: (i // nsb, 0, i % nsb, 0))
    pair_shape = jax.ShapeDtypeStruct((batch, N_PAIRS, seq, LANES), F32)
    vec = _const_spec((1, WIDTH))
    lora_in = _const_spec((D_MODEL, LORA))
    lora_out = _const_spec((LORA, WIDTH))
    return pl.pallas_call(
        functools.partial(_rwkv_proj_kernel, tiles_per_seq=nsb),
        grid=(m // tm,),
        in_specs=[
            pl.BlockSpec((tm, D_MODEL), lambda i: (i, 0)),
            pl.BlockSpec((8, D_MODEL), lambda i: (jnp.maximum(i * halo_blocks - 1, 0), 0)),
            _const_spec((1, D_MODEL)),
            _const_spec((6, D_MODEL)),
            _const_spec((4, D_MODEL, WIDTH)),
            vec, lora_in, lora_in, lora_out, lora_out,
            vec, lora_in, lora_in, lora_out, lora_out,
            vec, vec, vec,
        ],
        out_specs=[pair_spec] * 7 + [pl.BlockSpec((tm, WIDTH), lambda i: (i, 0))],
        out_shape=[pair_shape] * 7 + [jax.ShapeDtypeStruct((m, WIDTH), BF16)],
        compiler_params=_params(1),
        name="rwkv_proj",
    )(x2d, x2d, *weights)


def _unit_lower_inverse(low, eye, blockdiag):
    ld = jnp.where(blockdiag[8], low, 0.0)
    l2 = _mm3(ld, ld)
    l4 = _mm3(l2, l2)
    p1 = eye + ld + l2 + _mm3(ld, l2)
    t = p1 + _mm3(p1, l4)
    b = 8
    while b < CHUNK:
        coupling = jnp.where(jnp.logical_and(blockdiag[2 * b], jnp.logical_not(blockdiag[b])),
                             low, 0.0)
        t = t + _mm3(_mm3(t, coupling), t)
        b *= 2
    return t


def _rwkv_scan_kernel(r_ref, k_ref, v_ref, lw_ref, kk_ref, bb_ref, bonus_ref, lng_ref, lnb_ref,
                      o_ref, z_ref, *, chunks_per_step):
    C = CHUNK

    @pl.when(pl.program_id(1) == 0)
    def _():
        z_ref[...] = jnp.zeros_like(z_ref)

    ti = _iota((C, C), 0)
    si = _iota((C, C), 1)
    eye = (ti == si).astype(F32)
    tri_incl = (si <= ti).astype(BF16)
    blockdiag = {b: (ti // b) == (si // b) for b in (8, 16, 32, 64)}
    t2 = _iota((2 * C, 2 * C), 0)
    s2 = _iota((2 * C, 2 * C), 1) % C
    mask2 = jnp.logical_or(jnp.logical_and(t2 < C, s2 < t2),
                           jnp.logical_and(t2 >= C, s2 <= t2 - C))
    lane = _iota((C, LANES), 1)
    head_lo = lane < HEAD_DIM
    head_masks = (head_lo, jnp.logical_not(head_lo))
    upper_lanes = lane >= C
    ones_time = jnp.ones((C, LANES), BF16)
    head_bd = ((_iota((LANES, LANES), 0) // HEAD_DIM) == (_iota((LANES, LANES), 1) // HEAD_DIM))
    avg_bd = head_bd.astype(BF16) * (1.0 / HEAD_DIM)

    def pair_body(p, _):
        lng = lng_ref[p]
        lnb = lnb_ref[p]
        for c in range(chunks_per_step):
            rows = slice(c * C, (c + 1) * C)
            r = r_ref[0, p, rows, :]
            k = k_ref[0, p, rows, :]
            v = v_ref[0, p, rows, :]
            lw = lw_ref[0, p, rows, :]
            kk = kk_ref[0, p, rows, :]
            bb = bb_ref[0, p, rows, :]
            z = z_ref[p]

            lw_parts = _split3(lw)
            cs = sum(_dot(tri_incl, part) for part in lw_parts)
            decay_in = jnp.exp(cs)
            r_t = r * decay_in
            inv = jnp.exp(-cs)
            k_t = k * inv
            b_t = bb * inv
            a_t = -(kk * jnp.exp(cs - lw))
            bk = jnp.concatenate([b_t, k_t], axis=0)
            vv = jnp.concatenate([v, v], axis=0)

            state_terms = _mm3(jnp.concatenate([a_t, r_t], axis=0), z)
            az = state_terms[:C]
            rz = state_terms[C:]

            tinv, lower_rbk = [], []
            akv = None
            for hh in range(2):
                lhs = jnp.concatenate([jnp.where(head_masks[hh], a_t, 0.0),
                                       jnp.where(head_masks[hh], r_t, 0.0)], axis=0)
                m1 = jnp.where(mask2, _mm3(lhs, bk, _dot_nt), 0.0)
                abk = m1[:C]
                lower_rbk.append(m1[C:])
                tinv.append(_unit_lower_inverse(abk[:, :C], eye, blockdiag))
                akv_h = _mm3(jnp.where(upper_lanes, abk, 0.0), vv)
                akv = akv_h if hh == 0 else jnp.where(head_lo, akv, akv_h)
            rhs = az + akv
            u = jnp.where(head_lo, _mm3(tinv[0], rhs), _mm3(tinv[1], rhs))
            uv = jnp.concatenate([u, v], axis=0)
            y = rz + jnp.where(head_lo, _mm3(lower_rbk[0], uv), _mm3(lower_rbk[1], uv))

            decay_all = decay_in[C - 1:C, :]
            upd = _mm3(bk * decay_all, uv, _dot_tn)
            decay_cols = jnp.exp(sum(_dot_tn(part, ones_time) for part in lw_parts))
            z_ref[p] = decay_cols * z + jnp.where(head_bd, upd, 0.0)

            mu = _mm2(y, avg_bd)
            d = y - mu
            var = _mm2(d * d, avg_bd)
            yn = d * lax.rsqrt(var + LNX_EPS) * lng + lnb
            o_ref[0, p, rows, :] = (yn + bonus_ref[0, p, rows, :]).astype(o_ref.dtype)
        return 0

    lax.fori_loop(0, N_PAIRS, pair_body, 0)


def _rwkv_scan(r, k, v, lw, kk, bb, bonus, lnx_g, lnx_b, chunks_per_step):
    batch, _, seq, _ = r.shape
    tc = CHUNK * chunks_per_step
    blk = pl.BlockSpec((1, N_PAIRS, tc, LANES), lambda b, c: (b, 0, c, 0))
    vec = _const_spec((N_PAIRS, 1, LANES))
    return pl.pallas_call(
        functools.partial(_rwkv_scan_kernel, chunks_per_step=chunks_per_step),
        grid=(batch, seq // tc),
        in_specs=[blk] * 7 + [vec, vec],
        out_specs=blk,
        out_shape=jax.ShapeDtypeStruct(r.shape, BF16),
        scratch_shapes=[pltpu.VMEM((N_PAIRS, LANES, LANES), F32)],
        compiler_params=_params(2),
        name="rwkv_scan",
    )(r, k, v, lw, kk, bb, bonus, lnx_g, lnx_b)


def _hi_lo(w):
    hi = w.astype(BF16)
    lo = (w - hi.astype(F32)).astype(BF16)
    return hi, lo


def kernel(x, moba_norm_g, moba_w_in, moba_w_out, rwkv_norm_g, rwkv_mix, rwkv_w_in, rwkv_w0, rwkv_w1, rwkv_w2, rwkv_a0, rwkv_a1, rwkv_a2, rwkv_k_k, rwkv_k_a, rwkv_r_k, rwkv_lnx_g, rwkv_lnx_b, rwkv_w_out, final_norm_g):
    batch, seq, d = x.shape
    assert d == D_MODEL and seq % MOBA_BLOCK == 0
    x2d = x.reshape(batch * seq, d)
    row = lambda t: t.reshape(1, -1).astype(F32)

    w_hi, w_lo = _hi_lo(moba_w_in)
    q, k, v, sg = _moba_proj(x2d, row(moba_norm_g), w_hi, w_lo[:, :2 * WIDTH], batch, seq, tm=512)
    slopes = jnp.asarray([2.0 ** (-8.0 * (i + 1) / N_HEADS) for i in range(N_HEADS)], F32)
    o = _moba_attn(slopes, q, k, v)
    x1 = _out_proj(o, sg, x2d, moba_w_out.astype(BF16), row(final_norm_g), seq, tm=512,
                   final_norm=False)

    w1h, w1l = _hi_lo(rwkv_w1)
    w2h, w2l = _hi_lo(rwkv_w2)
    a1h, a1l = _hi_lo(rwkv_a1)
    a2h, a2l = _hi_lo(rwkv_a2)
    weights = (row(rwkv_norm_g), rwkv_mix.astype(F32), rwkv_w_in.astype(BF16),
               row(rwkv_w0), w1h, w1l, w2h, w2l, row(rwkv_a0), a1h, a1l, a2h, a2l,
               row(rwkv_k_k), row(rwkv_k_a), row(rwkv_r_k))
    r, k2, v2, lw, kk, bb, bonus, sg2 = _rwkv_proj(x1, weights, batch, seq, tm=256)
    pair_vec = lambda t: t.reshape(N_PAIRS, 1, LANES).astype(F32)
    o2 = _rwkv_scan(r, k2, v2, lw, kk, bb, bonus, pair_vec(rwkv_lnx_g), pair_vec(rwkv_lnx_b),
                    chunks_per_step=2)
    out = _out_proj(o2, sg2, x1, rwkv_w_out.astype(BF16), row(final_norm_g), seq, tm=512,
                    final_norm=True)
    return out.reshape(batch, seq, d)
```

```python
import functools
import math

import jax
import jax.numpy as jnp
from jax import lax
from jax.experimental import pallas as pl
from jax.experimental.pallas import tpu as pltpu

F32 = jnp.float32
BF16 = jnp.bfloat16

D_MODEL = 1024
N_HEADS = 16
HEAD_DIM = 64
WIDTH = N_HEADS * HEAD_DIM
LANES = 128
N_PAIRS = WIDTH // LANES
MOBA_BLOCK = 256
MOBA_TOPK = 3
LORA = 64
NORM_EPS = 1e-6
LNX_EPS = 64e-5
CHUNK = 64
VMEM_LIMIT = 56 * 1024 * 1024
NEG_INF = float("-inf")
POS_INF = float("inf")
LOG2E = math.log2(math.e)


def _dot(a, b):
    return jnp.dot(a, b, preferred_element_type=F32)


def _dot_nt(a, b):
    return lax.dot_general(a, b, (((1,), (1,)), ((), ())), preferred_element_type=F32)


def _dot_tn(a, b):
    return lax.dot_general(a, b, (((0,), (0,)), ((), ())), preferred_element_type=F32)


def _split2(x):
    hi = x.astype(BF16)
    lo = (x - hi.astype(F32)).astype(BF16)
    return hi, lo


def _split3(x):
    hi = x.astype(BF16)
    r1 = x - hi.astype(F32)
    mid = r1.astype(BF16)
    lo = (r1 - mid.astype(F32)).astype(BF16)
    return hi, mid, lo


def _cat(xs, axis):
    return jnp.concatenate(xs, axis=axis)


def _mm3_parts(ah, al, bh, bl, kind):
    if kind == "nn":
        return _dot(_cat([ah, al], 1), _cat([bh, bh], 0)) + _dot(ah, bl)
    if kind == "nt":
        return _dot_nt(_cat([ah, al], 1), _cat([bh, bh], 1)) + _dot_nt(ah, bl)
    assert kind == "tn"
    return _dot_tn(_cat([ah, al], 0), _cat([bh, bh], 0)) + _dot_tn(ah, bl)


def _mm3(a, b, kind="nn"):
    ah, al = _split2(a)
    bh, bl = _split2(b)
    return _mm3_parts(ah, al, bh, bl, kind)


def _mm2(a, b2_exact):
    ah, al = _split2(a)
    return _dot(_cat([ah, al], 1), b2_exact)


def _rms(x, g):
    ms = jnp.mean(x * x, axis=-1, keepdims=True)
    return x * lax.rsqrt(ms + NORM_EPS) * g


def _iota(shape, dim):
    return lax.broadcasted_iota(jnp.int32, shape, dim)


def _head_blockdiag(n):
    return (_iota((n, n), 0) // HEAD_DIM) == (_iota((n, n), 1) // HEAD_DIM)


def _const_spec(shape):
    nd = len(shape)
    return pl.BlockSpec(shape, lambda *_: (0,) * nd)


def _params(n_axes):
    return pltpu.CompilerParams(dimension_semantics=("arbitrary",) * n_axes,
                                vmem_limit_bytes=VMEM_LIMIT)


def _moba_proj_kernel(x_ref, g_ref, whi_ref, wlo_ref, q_ref, k_ref, v_ref, sg_ref):
    h = _rms(x_ref[...], g_ref[...])
    hh, hl = _split2(h)

    def proj3(slab):
        cols = slice(slab * WIDTH, (slab + 1) * WIDTH)
        return (_dot(hh, whi_ref[:, cols]) + _dot(hh, wlo_ref[:, cols])
                + _dot(hl, whi_ref[:, cols]))

    q = proj3(0)
    k = proj3(1)
    v = _dot(hh, whi_ref[:, 2 * WIDTH:3 * WIDTH])
    gate = _dot(hh, whi_ref[:, 3 * WIDTH:4 * WIDTH])
    for p in range(N_PAIRS):
        cols = slice(p * LANES, (p + 1) * LANES)
        q_ref[0, p] = q[:, cols]
        k_ref[0, p] = k[:, cols].astype(BF16)
        v_ref[0, p] = v[:, cols].astype(BF16)
    sg_ref[...] = (gate * jax.nn.sigmoid(gate)).astype(BF16)


def _moba_proj(x2d, norm_g, w_hi, w_lo, batch, seq, tm):
    m = x2d.shape[0]
    nsb = seq // tm
    pair_spec = pl.BlockSpec((1, N_PAIRS, tm, LANES), lambda i: (i // nsb, 0, i % nsb, 0))
    pair_shape = (batch, N_PAIRS, seq, LANES)
    return pl.pallas_call(
        _moba_proj_kernel,
        grid=(m // tm,),
        in_specs=[
            pl.BlockSpec((tm, D_MODEL), lambda i: (i, 0)),
            _const_spec((1, D_MODEL)),
            _const_spec((D_MODEL, 4 * WIDTH)),
            _const_spec((D_MODEL, 2 * WIDTH)),
        ],
        out_specs=[pair_spec, pair_spec, pair_spec,
                   pl.BlockSpec((tm, WIDTH), lambda i: (i, 0))],
        out_shape=[
            jax.ShapeDtypeStruct(pair_shape, F32),
            jax.ShapeDtypeStruct(pair_shape, BF16),
            jax.ShapeDtypeStruct(pair_shape, BF16),
            jax.ShapeDtypeStruct((m, WIDTH), BF16),
        ],
        compiler_params=_params(1),
        name="moba_proj",
    )(x2d, norm_g, w_hi, w_lo)


def _moba_attn_kernel(sfeat_ref, slopes_ref, q_ref, k_ref, v_ref, o_ref,
                      kmean_ref, kaug_ref, vt_ref, *, nb, unroll):
    L = MOBA_BLOCK
    p = pl.program_id(1)
    n = pl.program_id(2)
    lane = _iota((L, LANES), 1)
    head_lo = lane < HEAD_DIM
    head_masks = (head_lo, jnp.logical_not(head_lo))
    feat_base = (HEAD_DIM, 0)

    @pl.when(n == 0)
    def _():
        offs = _iota((L, LANES), 0).astype(F32)
        for j in range(nb):
            rows = slice(j * L, (j + 1) * L)
            kf = k_ref[0, 0, rows, :].astype(F32)
            kmean_ref[j:j + 1, :] = jnp.mean(kf, axis=0, keepdims=True)
            for hh in range(2):
                fb = feat_base[hh]
                feat = jnp.where(jnp.logical_or(lane == fb, lane == fb + 1), offs, 0.0)
                kaug_ref[hh, rows, :] = jnp.where(head_masks[hh], kf, feat).astype(BF16)
            vt_ref[j] = v_ref[0, 0, rows, :].astype(F32).T.astype(BF16)

    qf = q_ref[0, 0]
    qs = qf * (HEAD_DIM ** -0.5 * LOG2E)
    blk = _iota((nb, L), 0)
    kmean = kmean_ref[...]
    key_i = _iota((L, L), 0)
    qry_i = _iota((L, L), 1)
    start = pl.multiple_of(n * L, L)

    qa, bits, stats = [], [], []
    for hh in range(2):
        h = 2 * p + hh
        gate = _mm3(kmean, jnp.where(head_masks[hh], qf, 0.0), "nt")
        gate = jnp.where(blk < n, gate, NEG_INF)
        chosen_bits = jnp.zeros((1, L), jnp.int32)
        for _ in range(MOBA_TOPK):
            mx = jnp.max(gate, axis=0, keepdims=True)
            first = jnp.min(jnp.where(gate == mx, blk, nb), axis=0, keepdims=True)
            valid = mx > NEG_INF
            chosen_bits = chosen_bits | jnp.where(valid, jnp.left_shift(1, first), 0)
            gate = jnp.where(jnp.logical_and(blk == first, valid), NEG_INF, gate)
        bits.append(chosen_bits)
        fb = feat_base[hh]
        feat = jnp.where(lane == fb, sfeat_ref[2 * h],
                         jnp.where(lane == fb + 1, sfeat_ref[2 * h + 1], 0.0))
        qa.append(jnp.where(head_masks[hh], qs, feat).astype(BF16))

    def scores(j):
        off = pl.multiple_of(j * L, L)
        return [_dot_nt(kaug_ref[hh, pl.ds(off, L), :], qa[hh]) for hh in range(2)]

    def scores_group(i):
        out = []
        for u in range(unroll):
            out += scores(jnp.minimum(i * unroll + u, nb - 1))
        return out

    own = scores(n)
    ahead = scores_group(0)
    for hh in range(2):
        s = jnp.where(key_i <= qry_i, own[hh], NEG_INF)
        m = jnp.max(s, axis=0, keepdims=True)
        pr = jnp.exp2(s - m)
        l = jnp.sum(pr, axis=0, keepdims=True)
        acc = _dot(vt_ref[n, hh * HEAD_DIM:(hh + 1) * HEAD_DIM, :], pr.astype(BF16))
        stats += [m, l, acc]

    ns = 2 * unroll

    def body(i, carry):
        s_now = carry[:ns]
        nxt = scores_group(i + 1)
        run = list(carry[ns:])
        for u in range(unroll):
            j = i * unroll + u
            jc = jnp.minimum(j, nb - 1)
            in_range = j < n
            dist = ((j - n) * L).astype(F32) * LOG2E
            for hh in range(2):
                m, l, acc = run[3 * hh:3 * hh + 3]
                c = dist * slopes_ref[2 * p + hh]
                chosen = jnp.logical_and((jnp.right_shift(bits[hh], jc) & 1) == 1, in_range)
                s = s_now[2 * u + hh]
                m_new = jnp.maximum(m, jnp.where(chosen, jnp.max(s, axis=0, keepdims=True) + c,
                                                 NEG_INF))
                alpha = jnp.exp2(m - m_new)
                pr = jnp.exp2(s - jnp.where(chosen, m_new - c, POS_INF))
                l_new = alpha * l + jnp.sum(pr, axis=0, keepdims=True)
                pv = _dot(vt_ref[jc, hh * HEAD_DIM:(hh + 1) * HEAD_DIM, :], pr.astype(BF16))
                run[3 * hh:3 * hh + 3] = [m_new, l_new, alpha * acc + pv]
        return (*nxt, *run)

    res = lax.fori_loop(0, (n + unroll - 1) // unroll, body, (*ahead, *stats))
    m0, l0, acc0, m1, l1, acc1 = res[ns:]
    o_t = _cat([acc0 / l0, acc1 / l1], 0)
    o_ref[0, 0] = o_t.T.astype(o_ref.dtype)


def _moba_attn(sfeat, slopes, q, k, v):
    batch, _, seq, _ = q.shape
    nb = seq // MOBA_BLOCK
    tile = pl.BlockSpec((1, 1, MOBA_BLOCK, LANES), lambda b, p, n: (b, p, n, 0))
    full = pl.BlockSpec((1, 1, seq, LANES), lambda b, p, n: (b, p, 0, 0))
    smem = pl.BlockSpec(memory_space=pltpu.SMEM)
    return pl.pallas_call(
        functools.partial(_moba_attn_kernel, nb=nb, unroll=2),
        grid=(batch, N_PAIRS, nb),
        in_specs=[smem, smem, tile, full, full],
        out_specs=tile,
        out_shape=jax.ShapeDtypeStruct(q.shape, BF16),
        scratch_shapes=[pltpu.VMEM((nb, LANES), F32),
                        pltpu.VMEM((2, seq, LANES), BF16),
                        pltpu.VMEM((nb, LANES, MOBA_BLOCK), BF16)],
        compiler_params=_params(3),
        name="moba_attn",
    )(sfeat, slopes, q, k, v)


def _out_proj_kernel(o_ref, sg_ref, x_ref, w_ref, g_ref, y_ref, *, final_norm):
    o = _cat([o_ref[0, p] for p in range(N_PAIRS)], 1)
    gated = (o.astype(F32) * sg_ref[...].astype(F32)).astype(BF16)
    y = x_ref[...] + _dot(gated, w_ref[...])
    if final_norm:
        y = _rms(y, g_ref[...])
    y_ref[...] = y


def _out_proj(o_pairs, sg, x2d, w_bf16, norm_g, seq, tm, final_norm):
    m = x2d.shape[0]
    nsb = seq // tm
    row_spec = pl.BlockSpec((tm, D_MODEL), lambda i: (i, 0))
    return pl.pallas_call(
        functools.partial(_out_proj_kernel, final_norm=final_norm),
        grid=(m // tm,),
        in_specs=[
            pl.BlockSpec((1, N_PAIRS, tm, LANES), lambda i: (i // nsb, 0, i % nsb, 0)),
            row_spec, row_spec,
            _const_spec((WIDTH, D_MODEL)),
            _const_spec((1, D_MODEL)),
        ],
        out_specs=row_spec,
        out_shape=jax.ShapeDtypeStruct((m, D_MODEL), F32),
        compiler_params=_params(1),
        name="out_proj_final" if final_norm else "out_proj",
    )(o_pairs, sg, x2d, w_bf16, norm_g)


def _rwkv_proj_kernel(x_ref, halo_ref, g_ref, mix_ref, win_ref, w0_ref, w1h_ref, w1l_ref,
                      w2h_ref, w2l_ref, a0_ref, a1h_ref, a1l_ref, a2h_ref, a2l_ref,
                      kk_ref, ka_ref, rk_ref,
                      r_out, k_out, v_out, lw_out, kk_out, bb_out, bonus_out, sg_out,
                      *, tiles_per_seq):
    i = pl.program_id(0)
    tm = x_ref.shape[0]
    g = g_ref[...]
    h = _rms(x_ref[...], g)
    prev_row = _rms(halo_ref[...], g)[7:8, :]
    prev_row = jnp.where(i % tiles_per_seq == 0, 0.0, prev_row)
    rolled = pltpu.roll(h, 1, 0)
    h_prev = jnp.where(_iota((tm, D_MODEL), 0) == 0, prev_row, rolled)
    xx = h_prev - h

    def stream(n):
        return h + xx * mix_ref[n:n + 1, :]

    r = _dot(stream(0).astype(BF16), win_ref[0])
    k = _dot(stream(1).astype(BF16), win_ref[1])
    v = _dot(stream(2).astype(BF16), win_ref[2])
    gt = _dot(stream(3).astype(BF16), win_ref[3])
    sg_out[...] = (gt * jax.nn.sigmoid(gt)).astype(BF16)

    def lora(xs, ah_ref, al_ref, bh_ref, bl_ref, act):
        xh, xl = _split2(xs)
        mid = _dot(xh, ah_ref[...]) + _dot(xh, al_ref[...]) + _dot(xl, ah_ref[...])
        mid = act(mid)
        mh, ml = _split2(mid)
        return _dot(mh, bh_ref[...]) + _dot(mh, bl_ref[...]) + _dot(ml, bh_ref[...])

    z = -(w0_ref[...] + lora(stream(4), w1h_ref, w1l_ref, w2h_ref, w2l_ref, jnp.tanh))
    softplus = jnp.maximum(z, 0.0) + jnp.log(1.0 + jnp.exp(-jnp.abs(z)))
    lw = -jnp.exp(-softplus - 0.5)
    a = jax.nn.sigmoid(a0_ref[...] + lora(stream(5), a1h_ref, a1l_ref, a2h_ref, a2l_ref,
                                          lambda t: t))
    kr = k * kk_ref[...]
    k_mod = k * (1.0 + (a - 1.0) * ka_ref[...])
    rk = r * k_mod * rk_ref[...]
    ones_bd = _head_blockdiag(LANES).astype(BF16)
    ones_bd2 = _cat([ones_bd, ones_bd], 0)
    for p in range(N_PAIRS):
        cols = slice(p * LANES, (p + 1) * LANES)
        kr_p = kr[:, cols]
        ss = _mm2(kr_p * kr_p, ones_bd2)
        kk_p = kr_p / jnp.maximum(jnp.sqrt(ss), 1e-12)
        r_out[0, p] = r[:, cols]
        k_out[0, p] = k_mod[:, cols]
        v_out[0, p] = v[:, cols]
        lw_out[0, p] = lw[:, cols]
        kk_out[0, p] = kk_p
        bb_out[0, p] = kk_p * a[:, cols]
        bonus_out[0, p] = _mm2(rk[:, cols], ones_bd2) * v[:, cols]


def _rwkv_proj(x2d, weights, batch, seq, tm):
    m = x2d.shape[0]
    nsb = seq // tm
    halo_blocks = tm // 8
    pair_spec = pl.BlockSpec((1, N_PAIRS, tm, LANES), lambda i: (i // nsb, 0, i % nsb, 0))
    pair_shape = jax.ShapeDtypeStruct((batch, N_PAIRS, seq, LANES), F32)
    vec = _const_spec((1, WIDTH))
    lora_in = _const_spec((D_MODEL, LORA))
    lora_out = _const_spec((LORA, WIDTH))
    return pl.pallas_call(
        functools.partial(_rwkv_proj_kernel, tiles_per_seq=nsb),
        grid=(m // tm,),
        in_specs=[
            pl.BlockSpec((tm, D_MODEL), lambda i: (i, 0)),
            pl.BlockSpec((8, D_MODEL), lambda i: (jnp.maximum(i * halo_blocks - 1, 0), 0)),
            _const_spec((1, D_MODEL)),
            _const_spec((6, D_MODEL)),
            _const_spec((4, D_MODEL, WIDTH)),
            vec, lora_in, lora_in, lora_out, lora_out,
            vec, lora_in, lora_in, lora_out, lora_out,
            vec, vec, vec,
        ],
        out_specs=[pair_spec] * 7 + [pl.BlockSpec((tm, WIDTH), lambda i: (i, 0))],
        out_shape=[pair_shape] * 7 + [jax.ShapeDtypeStruct((m, WIDTH), BF16)],
        compiler_params=_params(1),
        name="rwkv_proj",
    )(x2d, x2d, *weights)


def _rwkv_scan_kernel(r_ref, k_ref, v_ref, lw_ref, kk_ref, bb_ref, bonus_ref, lng_ref, lnb_ref,
                      o_ref, z_ref, *, chunks_per_step):
    C = CHUNK
    pairs = range(N_PAIRS)

    @pl.when(pl.program_id(1) == 0)
    def _():
        z_ref[...] = jnp.zeros_like(z_ref)

    row = _iota((C, LANES), 0)
    lane = _iota((C, LANES), 1)
    col = lane % C
    head_lo = lane < HEAD_DIM
    keep_lo = head_lo.astype(BF16)
    keep_hi = jnp.logical_not(head_lo).astype(BF16)
    eye = (row == col).astype(F32)
    strict = col < row
    lower = col <= row
    same_block = {b: (row // b) == (col // b) for b in (8, 16, 32)}
    tri3 = ((_iota((C, 3 * C), 1) % C) <= _iota((C, 3 * C), 0)).astype(BF16)
    ones3 = jnp.ones((3 * C, LANES), BF16)
    head_bd = _head_blockdiag(LANES)
    avg_bd = head_bd.astype(BF16) * (1.0 / HEAD_DIM)
    avg_bd2 = _cat([avg_bd, avg_bd], 0)

    def blockdiag(xb):
        return _cat([xb * keep_lo, xb * keep_hi], 0)

    def prod(a_list, b_list):
        parts = [(_split2(a), _split2(b)) for a, b in zip(a_list, b_list)]
        return [_mm3_parts(ah, al, blockdiag(bh), blockdiag(bl), "nn")
                for (ah, al), (bh, bl) in parts]

    z = [z_ref[p] for p in pairs]
    for c in range(chunks_per_step):
        rows = slice(c * C, (c + 1) * C)
        v = [v_ref[0, p, rows, :] for p in pairs]
        lw = [lw_ref[0, p, rows, :] for p in pairs]
        lw3 = [_cat(_split3(x), 0) for x in lw]
        cs = [_dot(tri3, x) for x in lw3]
        decay_in = [jnp.exp(x) for x in cs]
        inv = [jnp.exp(-x) for x in cs]
        r_t = [r_ref[0, p, rows, :] * decay_in[p] for p in pairs]
        k_t = [k_ref[0, p, rows, :] * inv[p] for p in pairs]
        b_t = [bb_ref[0, p, rows, :] * inv[p] for p in pairs]
        a_t = [-(kk_ref[0, p, rows, :] * jnp.exp(cs[p] - lw[p])) for p in pairs]
        ar = [_split2(_cat([a_t[p], r_t[p]], 0)) for p in pairs]
        bsp = [_split2(x) for x in b_t]
        ksp = [_split2(x) for x in k_t]
        mb = [_mm3_parts(*ar[p], blockdiag(bsp[p][0]), blockdiag(bsp[p][1]), "nt") for p in pairs]
        mk = [_mm3_parts(*ar[p], blockdiag(ksp[p][0]), blockdiag(ksp[p][1]), "nt") for p in pairs]
        low = [jnp.where(strict, x[:C], 0.0) for x in mb]
        ak = [jnp.where(strict, x[:C], 0.0) for x in mk]
        rbk = [_cat([jnp.where(lower, mb[p][C:], 0.0), jnp.where(lower, mk[p][C:], 0.0)], 1)
               for p in pairs]

        ld = [jnp.where(same_block[8], x, 0.0) for x in low]
        l2 = prod(ld, ld)
        l4 = prod(l2, l2)
        l3 = prod(ld, l2)
        p1 = [eye + ld[p] + l2[p] + l3[p] for p in pairs]
        p1l4 = prod(p1, l4)
        t = [p1[p] + p1l4[p] for p in pairs]
        b = 8
        while b < C:
            couple = jnp.logical_not(same_block[b])
            if 2 * b < C:
                couple = jnp.logical_and(same_block[2 * b], couple)
            x = [jnp.where(couple, y, 0.0) for y in low]
            tx = prod(t, x)
            txt = prod(tx, t)
            t = [t[p] + txt[p] for p in pairs]
            b *= 2

        akv = prod(ak, v)
        zs = [_split2(x) for x in z]
        st = [_mm3_parts(*ar[p], *zs[p], "nn") for p in pairs]
        rhs = [st[p][:C] + akv[p] for p in pairs]
        u = prod(t, rhs)
        usp = [_split2(x) for x in u]
        vsp = [_split2(x) for x in v]
        uvh = [_cat([blockdiag(usp[p][0]), blockdiag(vsp[p][0])], 0) for p in pairs]
        uvl = [_cat([blockdiag(usp[p][1]), blockdiag(vsp[p][1])], 0) for p in pairs]
        rsp = [_split2(x) for x in rbk]
        y = [st[p][C:] + _mm3_parts(*rsp[p], uvh[p], uvl[p], "nn") for p in pairs]

        upd = []
        for p in pairs:
            scale = decay_in[p][C - 1:C, :]
            bkh, bkl = _split2(_cat([b_t[p] * scale, k_t[p] * scale], 0))
            upd.append(_mm3_parts(bkh, bkl, _cat([usp[p][0], vsp[p][0]], 0),
                                  _cat([usp[p][1], vsp[p][1]], 0), "tn"))
        decay_cols = [jnp.exp(_dot_tn(x, ones3)) for x in lw3]
        z = [decay_cols[p] * z[p] + jnp.where(head_bd, upd[p], 0.0) for p in pairs]

        mu = [_mm2(x, avg_bd2) for x in y]
        d = [y[p] - mu[p] for p in pairs]
        var = [_mm2(x * x, avg_bd2) for x in d]
        for p in pairs:
            yn = d[p] * lax.rsqrt(var[p] + LNX_EPS) * lng_ref[p] + lnb_ref[p]
            o_ref[0, p, rows, :] = (yn + bonus_ref[0, p, rows, :]).astype(o_ref.dtype)
    for p in pairs:
        z_ref[p] = z[p]


def _rwkv_scan(r, k, v, lw, kk, bb, bonus, lnx_g, lnx_b, chunks_per_step):
    batch, _, seq, _ = r.shape
    tc = CHUNK * chunks_per_step
    blk = pl.BlockSpec((1, N_PAIRS, tc, LANES), lambda b, c: (b, 0, c, 0))
    vec = _const_spec((N_PAIRS, 1, LANES))
    return pl.pallas_call(
        functools.partial(_rwkv_scan_kernel, chunks_per_step=chunks_per_step),
        grid=(batch, seq // tc),
        in_specs=[blk] * 7 + [vec, vec],
        out_specs=blk,
        out_shape=jax.ShapeDtypeStruct(r.shape, BF16),
        scratch_shapes=[pltpu.VMEM((N_PAIRS, LANES, LANES), F32)],
        compiler_params=_params(2),
        name="rwkv_scan",
    )(r, k, v, lw, kk, bb, bonus, lnx_g, lnx_b)


def _hi_lo(w):
    hi = w.astype(BF16)
    lo = (w - hi.astype(F32)).astype(BF16)
    return hi, lo


def kernel(x, moba_norm_g, moba_w_in, moba_w_out, rwkv_norm_g, rwkv_mix, rwkv_w_in, rwkv_w0, rwkv_w1, rwkv_w2, rwkv_a0, rwkv_a1, rwkv_a2, rwkv_k_k, rwkv_k_a, rwkv_r_k, rwkv_lnx_g, rwkv_lnx_b, rwkv_w_out, final_norm_g):
    batch, seq, d = x.shape
    assert d == D_MODEL and seq % MOBA_BLOCK == 0 and seq // MOBA_BLOCK <= 32
    x2d = x.reshape(batch * seq, d)
    row = lambda t: t.reshape(1, -1).astype(F32)

    w_hi, w_lo = _hi_lo(moba_w_in)
    q, k, v, sg = _moba_proj(x2d, row(moba_norm_g), w_hi, w_lo[:, :2 * WIDTH], batch, seq, tm=512)
    slopes = jnp.asarray([2.0 ** (-8.0 * (i + 1) / N_HEADS) for i in range(N_HEADS)], F32)
    slope_hi = (slopes * LOG2E).astype(BF16).astype(F32)
    sfeat = jnp.stack([slope_hi, slopes * LOG2E - slope_hi], axis=1).reshape(-1)
    o = _moba_attn(sfeat, slopes, q, k, v)
    x1 = _out_proj(o, sg, x2d, moba_w_out.astype(BF16), row(final_norm_g), seq, tm=512,
                   final_norm=False)

    w1h, w1l = _hi_lo(rwkv_w1)
    w2h, w2l = _hi_lo(rwkv_w2)
    a1h, a1l = _hi_lo(rwkv_a1)
    a2h, a2l = _hi_lo(rwkv_a2)
    weights = (row(rwkv_norm_g), rwkv_mix.astype(F32), rwkv_w_in.astype(BF16),
               row(rwkv_w0), w1h, w1l, w2h, w2l, row(rwkv_a0), a1h, a1l, a2h, a2l,
               row(rwkv_k_k), row(rwkv_k_a), row(rwkv_r_k))
    r, k2, v2, lw, kk, bb, bonus, sg2 = _rwkv_proj(x1, weights, batch, seq, tm=256)
    pair_vec = lambda t: t.reshape(N_PAIRS, 1, LANES).astype(F32)
    o2 = _rwkv_scan(r, k2, v2, lw, kk, bb, bonus, pair_vec(rwkv_lnx_g), pair_vec(rwkv_lnx_b),
                    chunks_per_step=1)
    out = _out_proj(o2, sg2, x1, rwkv_w_out.astype(BF16), row(final_norm_g), seq, tm=512,
                    final_norm=True)
    return out.reshape(batch, seq, d)
```

```python
import functools
import math

import jax
import jax.numpy as jnp
from jax import lax
from jax.experimental import pallas as pl
from jax.experimental.pallas import tpu as pltpu

F32 = jnp.float32
BF16 = jnp.bfloat16

D_MODEL = 1024
N_HEADS = 16
HEAD_DIM = 64
WIDTH = N_HEADS * HEAD_DIM
LANES = 128
N_PAIRS = WIDTH // LANES
MOBA_BLOCK = 256
MOBA_TOPK = 3
BF16_SUBLANES = 16
V_ROWS = HEAD_DIM + BF16_SUBLANES
LORA = 64
NORM_EPS = 1e-6
LNX_EPS = 64e-5
CHUNK = 64
VMEM_LIMIT = 56 * 1024 * 1024
NEG_INF = float("-inf")
POS_INF = float("inf")
LOG2E = math.log2(math.e)


def _dot(a, b):
    return jnp.dot(a, b, preferred_element_type=F32)


def _dot_nt(a, b):
    return lax.dot_general(a, b, (((1,), (1,)), ((), ())), preferred_element_type=F32)


def _dot_tn(a, b):
    return lax.dot_general(a, b, (((0,), (0,)), ((), ())), preferred_element_type=F32)


def _split2(x):
    hi = x.astype(BF16)
    lo = (x - hi.astype(F32)).astype(BF16)
    return hi, lo


def _split3(x):
    hi = x.astype(BF16)
    r1 = x - hi.astype(F32)
    mid = r1.astype(BF16)
    lo = (r1 - mid.astype(F32)).astype(BF16)
    return hi, mid, lo


def _cat(xs, axis):
    return jnp.concatenate(xs, axis=axis)


def _mm3_parts(ah, al, bh, bl, kind):
    if kind == "nn":
        return _dot(_cat([ah, al], 1), _cat([bh, bh], 0)) + _dot(ah, bl)
    if kind == "nt":
        return _dot_nt(_cat([ah, al], 1), _cat([bh, bh], 1)) + _dot_nt(ah, bl)
    assert kind == "tn"
    return _dot_tn(_cat([ah, al], 0), _cat([bh, bh], 0)) + _dot_tn(ah, bl)


def _mm3(a, b, kind="nn"):
    ah, al = _split2(a)
    bh, bl = _split2(b)
    return _mm3_parts(ah, al, bh, bl, kind)


def _mm2(a, b2_exact):
    ah, al = _split2(a)
    return _dot(_cat([ah, al], 1), b2_exact)


def _rms(x, g):
    ms = jnp.mean(x * x, axis=-1, keepdims=True)
    return x * lax.rsqrt(ms + NORM_EPS) * g


def _iota(shape, dim):
    return lax.broadcasted_iota(jnp.int32, shape, dim)


def _head_blockdiag(n):
    return (_iota((n, n), 0) // HEAD_DIM) == (_iota((n, n), 1) // HEAD_DIM)


def _const_spec(shape):
    nd = len(shape)
    return pl.BlockSpec(shape, lambda *_: (0,) * nd)


def _params(n_axes):
    return pltpu.CompilerParams(dimension_semantics=("arbitrary",) * n_axes,
                                vmem_limit_bytes=VMEM_LIMIT)


def _moba_proj_kernel(x_ref, g_ref, whi_ref, wlo_ref, q_ref, k_ref, v_ref, sg_ref):
    h = _rms(x_ref[...], g_ref[...])
    hh, hl = _split2(h)

    def proj3(slab):
        cols = slice(slab * WIDTH, (slab + 1) * WIDTH)
        return (_dot(hh, whi_ref[:, cols]) + _dot(hh, wlo_ref[:, cols])
                + _dot(hl, whi_ref[:, cols]))

    q = proj3(0)
    k = proj3(1)
    v = _dot(hh, whi_ref[:, 2 * WIDTH:3 * WIDTH])
    gate = _dot(hh, whi_ref[:, 3 * WIDTH:4 * WIDTH])
    for p in range(N_PAIRS):
        cols = slice(p * LANES, (p + 1) * LANES)
        q_ref[0, p] = q[:, cols]
        k_ref[0, p] = k[:, cols].astype(BF16)
        v_ref[0, p] = v[:, cols].astype(BF16)
    sg_ref[...] = (gate * jax.nn.sigmoid(gate)).astype(BF16)


def _moba_proj(x2d, norm_g, w_hi, w_lo, batch, seq, tm):
    m = x2d.shape[0]
    nsb = seq // tm
    pair_spec = pl.BlockSpec((1, N_PAIRS, tm, LANES), lambda i: (i // nsb, 0, i % nsb, 0))
    pair_shape = (batch, N_PAIRS, seq, LANES)
    return pl.pallas_call(
        _moba_proj_kernel,
        grid=(m // tm,),
        in_specs=[
            pl.BlockSpec((tm, D_MODEL), lambda i: (i, 0)),
            _const_spec((1, D_MODEL)),
            _const_spec((D_MODEL, 4 * WIDTH)),
            _const_spec((D_MODEL, 2 * WIDTH)),
        ],
        out_specs=[pair_spec, pair_spec, pair_spec,
                   pl.BlockSpec((tm, WIDTH), lambda i: (i, 0))],
        out_shape=[
            jax.ShapeDtypeStruct(pair_shape, F32),
            jax.ShapeDtypeStruct(pair_shape, BF16),
            jax.ShapeDtypeStruct(pair_shape, BF16),
            jax.ShapeDtypeStruct((m, WIDTH), BF16),
        ],
        compiler_params=_params(1),
        name="moba_proj",
    )(x2d, norm_g, w_hi, w_lo)


def _moba_attn_kernel(sfeat_ref, slopes_ref, q_ref, k_ref, v_ref, o_ref,
                      kmean_ref, kaug_ref, vt_ref, s_ref, acc_ref, *, nb, unroll):
    L = MOBA_BLOCK
    p = pl.program_id(1)
    n = pl.program_id(2)
    lane = _iota((L, LANES), 1)
    head_lo = lane < HEAD_DIM
    head_masks = (head_lo, jnp.logical_not(head_lo))
    feat_base = (HEAD_DIM, 0)

    @pl.when(n == 0)
    def _():
        offs = _iota((L, LANES), 0).astype(F32)
        ones_row = (_iota((V_ROWS - HEAD_DIM, L), 0) == 0).astype(BF16)
        for j in range(nb):
            rows = slice(j * L, (j + 1) * L)
            kf = k_ref[0, 0, rows, :].astype(F32)
            kmean_ref[j:j + 1, :] = jnp.mean(kf, axis=0, keepdims=True)
            for hh in range(2):
                fb = feat_base[hh]
                feat = jnp.where(jnp.logical_or(lane == fb, lane == fb + 1), offs, 0.0)
                kaug_ref[hh, rows, :] = jnp.where(head_masks[hh], kf, feat).astype(BF16)
            v_t = v_ref[0, 0, rows, :].astype(F32).T.astype(BF16)
            for hh in range(2):
                vt_ref[j, hh, :HEAD_DIM, :] = v_t[hh * HEAD_DIM:(hh + 1) * HEAD_DIM, :]
                vt_ref[j, hh, HEAD_DIM:, :] = ones_row

    qf = q_ref[0, 0]
    qs = qf * (HEAD_DIM ** -0.5 * LOG2E)
    blk = _iota((nb, L), 0)
    kmean = kmean_ref[...]
    key_i = _iota((L, L), 0)
    qry_i = _iota((L, L), 1)
    start = pl.multiple_of(n * L, L)

    qa, bits = [], []
    for hh in range(2):
        h = 2 * p + hh
        gate = _mm3(kmean, jnp.where(head_masks[hh], qf, 0.0), "nt")
        gate = jnp.where(blk < n, gate, NEG_INF)
        chosen_bits = jnp.zeros((1, L), jnp.int32)
        for _ in range(MOBA_TOPK):
            mx = jnp.max(gate, axis=0, keepdims=True)
            first = jnp.min(jnp.where(gate == mx, blk, nb), axis=0, keepdims=True)
            valid = mx > NEG_INF
            chosen_bits = chosen_bits | jnp.where(valid, jnp.left_shift(1, first), 0)
            gate = jnp.where(jnp.logical_and(blk == first, valid), NEG_INF, gate)
        bits.append(chosen_bits)
        fb = feat_base[hh]
        feat = jnp.where(lane == fb, sfeat_ref[2 * h],
                         jnp.where(lane == fb + 1, sfeat_ref[2 * h + 1], 0.0))
        qa.append(jnp.where(head_masks[hh], qs, feat).astype(BF16))

    def scores(j):
        off = pl.multiple_of(j * L, L)
        return [_dot_nt(kaug_ref[hh, pl.ds(off, L), :], qa[hh]) for hh in range(2)]

    def issue_scores(g, slot):
        maxes = []
        for u in range(unroll):
            for hh, s in enumerate(scores(jnp.minimum(g * unroll + u, nb - 1))):
                s_ref[slot, 2 * u + hh] = s
                maxes.append(jnp.max(s, axis=0, keepdims=True))
        return maxes

    def consume(g, slot, maxes, ms):
        ms = list(ms)
        for u in range(unroll):
            j = g * unroll + u
            jc = jnp.minimum(j, nb - 1)
            in_range = j < n
            dist = ((j - n) * L).astype(F32) * LOG2E
            for hh in range(2):
                c = dist * slopes_ref[2 * p + hh]
                chosen = jnp.logical_and((jnp.right_shift(bits[hh], jc) & 1) == 1, in_range)
                m_new = jnp.maximum(ms[hh], jnp.where(chosen, maxes[2 * u + hh] + c, NEG_INF))
                alpha = jnp.exp2(ms[hh] - m_new)
                shift = jnp.where(chosen, m_new - c, POS_INF)
                pr = jnp.exp2(s_ref[slot, 2 * u + hh] - shift).astype(BF16)
                acc_ref[hh] = alpha * acc_ref[hh] + _dot(vt_ref[jc, hh], pr)
                ms[hh] = m_new
        return ms

    own = scores(n)
    first_maxes = issue_scores(0, 0)
    ms = []
    for hh in range(2):
        s = jnp.where(key_i <= qry_i, own[hh], NEG_INF)
        m = jnp.max(s, axis=0, keepdims=True)
        pr = jnp.exp2(s - m).astype(BF16)
        acc_ref[hh] = _dot(vt_ref[n, hh], pr)
        ms.append(m)

    nm = 2 * unroll

    def body(i, carry):
        maxes, ms = carry[:nm], carry[nm:]
        maxes_b = issue_scores(2 * i + 1, 1)
        ms = consume(2 * i, 0, maxes, ms)
        maxes_a = issue_scores(2 * i + 2, 0)
        ms = consume(2 * i + 1, 1, maxes_b, ms)
        return (*maxes_a, *ms)

    per_trip = 2 * unroll
    lax.fori_loop(0, (n + per_trip - 1) // per_trip, body, (*first_maxes, *ms))
    acc0, acc1 = acc_ref[0], acc_ref[1]
    o_t = _cat([acc0[:HEAD_DIM] / acc0[HEAD_DIM:HEAD_DIM + 1],
                acc1[:HEAD_DIM] / acc1[HEAD_DIM:HEAD_DIM + 1]], 0)
    o_ref[0, 0] = o_t.T.astype(o_ref.dtype)


def _moba_attn(sfeat, slopes, q, k, v, unroll):
    batch, _, seq, _ = q.shape
    nb = seq // MOBA_BLOCK
    tile = pl.BlockSpec((1, 1, MOBA_BLOCK, LANES), lambda b, p, n: (b, p, n, 0))
    full = pl.BlockSpec((1, 1, seq, LANES), lambda b, p, n: (b, p, 0, 0))
    smem = pl.BlockSpec(memory_space=pltpu.SMEM)
    return pl.pallas_call(
        functools.partial(_moba_attn_kernel, nb=nb, unroll=unroll),
        grid=(batch, N_PAIRS, nb),
        in_specs=[smem, smem, tile, full, full],
        out_specs=tile,
        out_shape=jax.ShapeDtypeStruct(q.shape, BF16),
        scratch_shapes=[pltpu.VMEM((nb, LANES), F32),
                        pltpu.VMEM((2, seq, LANES), BF16),
                        pltpu.VMEM((nb, 2, V_ROWS, MOBA_BLOCK), BF16),
                        pltpu.VMEM((2, 2 * unroll, MOBA_BLOCK, MOBA_BLOCK), F32),
                        pltpu.VMEM((2, V_ROWS, MOBA_BLOCK), F32)],
        compiler_params=_params(3),
        name="moba_attn",
    )(sfeat, slopes, q, k, v)


def _out_proj_kernel(o_ref, sg_ref, x_ref, w_ref, g_ref, y_ref, *, final_norm):
    o = _cat([o_ref[0, p] for p in range(N_PAIRS)], 1)
    gated = (o.astype(F32) * sg_ref[...].astype(F32)).astype(BF16)
    y = x_ref[...] + _dot(gated, w_ref[...])
    if final_norm:
        y = _rms(y, g_ref[...])
    y_ref[...] = y


def _out_proj(o_pairs, sg, x2d, w_bf16, norm_g, seq, tm, final_norm):
    m = x2d.shape[0]
    nsb = seq // tm
    row_spec = pl.BlockSpec((tm, D_MODEL), lambda i: (i, 0))
    return pl.pallas_call(
        functools.partial(_out_proj_kernel, final_norm=final_norm),
        grid=(m // tm,),
        in_specs=[
            pl.BlockSpec((1, N_PAIRS, tm, LANES), lambda i: (i // nsb, 0, i % nsb, 0)),
            row_spec, row_spec,
            _const_spec((WIDTH, D_MODEL)),
            _const_spec((1, D_MODEL)),
        ],
        out_specs=row_spec,
        out_shape=jax.ShapeDtypeStruct((m, D_MODEL), F32),
        compiler_params=_params(1),
        name="out_proj_final" if final_norm else "out_proj",
    )(o_pairs, sg, x2d, w_bf16, norm_g)


def _rwkv_proj_kernel(x_ref, halo_ref, g_ref, mix_ref, win_ref, w0_ref, w1h_ref, w1l_ref,
                      w2h_ref, w2l_ref, a0_ref, a1h_ref, a1l_ref, a2h_ref, a2l_ref,
                      kk_ref, ka_ref, rk_ref,
                      r_out, k_out, v_out, lw_out, kk_out, bb_out, bonus_out, sg_out,
                      *, tiles_per_seq):
    i = pl.program_id(0)
    tm = x_ref.shape[0]
    g = g_ref[...]
    h = _rms(x_ref[...], g)
    prev_row = _rms(halo_ref[...], g)[7:8, :]
    prev_row = jnp.where(i % tiles_per_seq == 0, 0.0, prev_row)
    rolled = pltpu.roll(h, 1, 0)
    h_prev = jnp.where(_iota((tm, D_MODEL), 0) == 0, prev_row, rolled)
    xx = h_prev - h

    def stream(n):
        return h + xx * mix_ref[n:n + 1, :]

    r = _dot(stream(0).astype(BF16), win_ref[0])
    k = _dot(stream(1).astype(BF16), win_ref[1])
    v = _dot(stream(2).astype(BF16), win_ref[2])
    gt = _dot(stream(3).astype(BF16), win_ref[3])
    sg_out[...] = (gt * jax.nn.sigmoid(gt)).astype(BF16)

    def lora(xs, ah_ref, al_ref, bh_ref, bl_ref, act):
        xh, xl = _split2(xs)
        mid = _dot(xh, ah_ref[...]) + _dot(xh, al_ref[...]) + _dot(xl, ah_ref[...])
        mid = act(mid)
        mh, ml = _split2(mid)
        return _dot(mh, bh_ref[...]) + _dot(mh, bl_ref[...]) + _dot(ml, bh_ref[...])

    z = -(w0_ref[...] + lora(stream(4), w1h_ref, w1l_ref, w2h_ref, w2l_ref, jnp.tanh))
    softplus = jnp.maximum(z, 0.0) + jnp.log(1.0 + jnp.exp(-jnp.abs(z)))
    lw = -jnp.exp(-softplus - 0.5)
    a = jax.nn.sigmoid(a0_ref[...] + lora(stream(5), a1h_ref, a1l_ref, a2h_ref, a2l_ref,
                                          lambda t: t))
    kr = k * kk_ref[...]
    k_mod = k * (1.0 + (a - 1.0) * ka_ref[...])
    rk = r * k_mod * rk_ref[...]
    ones_bd = _head_blockdiag(LANES).astype(BF16)
    ones_bd2 = _cat([ones_bd, ones_bd], 0)
    for p in range(N_PAIRS):
        cols = slice(p * LANES, (p + 1) * LANES)
        kr_p = kr[:, cols]
        ss = _mm2(kr_p * kr_p, ones_bd2)
        kk_p = kr_p / jnp.maximum(jnp.sqrt(ss), 1e-12)
        r_out[0, p] = r[:, cols]
        k_out[0, p] = k_mod[:, cols]
        v_out[0, p] = v[:, cols]
        lw_out[0, p] = lw[:, cols]
        kk_out[0, p] = kk_p
        bb_out[0, p] = kk_p * a[:, cols]
        bonus_out[0, p] = _mm2(rk[:, cols], ones_bd2) * v[:, cols]


def _rwkv_proj(x2d, weights, batch, seq, tm):
    m = x2d.shape[0]
    nsb = seq // tm
    halo_blocks = tm // 8
    pair_spec = pl.BlockSpec((1, N_PAIRS, tm, LANES), lambda i: (i // nsb, 0, i % nsb, 0))
    pair_shape = jax.ShapeDtypeStruct((batch, N_PAIRS, seq, LANES), F32)
    vec = _const_spec((1, WIDTH))
    lora_in = _const_spec((D_MODEL, LORA))
    lora_out = _const_spec((LORA, WIDTH))
    return pl.pallas_call(
        functools.partial(_rwkv_proj_kernel, tiles_per_seq=nsb),
        grid=(m // tm,),
        in_specs=[
            pl.BlockSpec((tm, D_MODEL), lambda i: (i, 0)),
            pl.BlockSpec((8, D_MODEL), lambda i: (jnp.maximum(i * halo_blocks - 1, 0), 0)),
            _const_spec((1, D_MODEL)),
            _const_spec((6, D_MODEL)),
            _const_spec((4, D_MODEL, WIDTH)),
            vec, lora_in, lora_in, lora_out, lora_out,
            vec, lora_in, lora_in, lora_out, lora_out,
            vec, vec, vec,
        ],
        out_specs=[pair_spec] * 7 + [pl.BlockSpec((tm, WIDTH), lambda i: (i, 0))],
        out_shape=[pair_shape] * 7 + [jax.ShapeDtypeStruct((m, WIDTH), BF16)],
        compiler_params=_params(1),
        name="rwkv_proj",
    )(x2d, x2d, *weights)


def _rwkv_scan_kernel(r_ref, k_ref, v_ref, lw_ref, kk_ref, bb_ref, bonus_ref, lng_ref, lnb_ref,
                      o_ref, z_ref, *, chunks_per_step):
    C = CHUNK
    pairs = range(N_PAIRS)

    @pl.when(pl.program_id(1) == 0)
    def _():
        z_ref[...] = jnp.zeros_like(z_ref)

    row = _iota((C, LANES), 0)
    lane = _iota((C, LANES), 1)
    col = lane % C
    head_lo = lane < HEAD_DIM
    keep_lo = head_lo.astype(BF16)
    keep_hi = jnp.logical_not(head_lo).astype(BF16)
    eye = (row == col).astype(F32)
    strict = col < row
    lower = col <= row
    same_block = {b: (row // b) == (col // b) for b in (8, 16, 32)}
    tri3 = ((_iota((C, 3 * C), 1) % C) <= _iota((C, 3 * C), 0)).astype(BF16)
    ones3 = jnp.ones((3 * C, LANES), BF16)
    head_bd = _head_blockdiag(LANES)
    avg_bd = head_bd.astype(BF16) * (1.0 / HEAD_DIM)
    avg_bd2 = _cat([avg_bd, avg_bd], 0)

    def blockdiag(xb):
        return _cat([xb * keep_lo, xb * keep_hi], 0)

    def prod(a_list, b_list):
        parts = [(_split2(a), _split2(b)) for a, b in zip(a_list, b_list)]
        return [_mm3_parts(ah, al, blockdiag(bh), blockdiag(bl), "nn")
                for (ah, al), (bh, bl) in parts]

    z = [z_ref[p] for p in pairs]
    for c in range(chunks_per_step):
        rows = slice(c * C, (c + 1) * C)
        v = [v_ref[0, p, rows, :] for p in pairs]
        lw = [lw_ref[0, p, rows, :] for p in pairs]
        lw3 = [_cat(_split3(x), 0) for x in lw]
        cs = [_dot(tri3, x) for x in lw3]
        decay_in = [jnp.exp(x) for x in cs]
        inv = [jnp.exp(-x) for x in cs]
        r_t = [r_ref[0, p, rows, :] * decay_in[p] for p in pairs]
        k_t = [k_ref[0, p, rows, :] * inv[p] for p in pairs]
        b_t = [bb_ref[0, p, rows, :] * inv[p] for p in pairs]
        a_t = [-(kk_ref[0, p, rows, :] * jnp.exp(cs[p] - lw[p])) for p in pairs]
        ar = [_split2(_cat([a_t[p], r_t[p]], 0)) for p in pairs]
        bsp = [_split2(x) for x in b_t]
        ksp = [_split2(x) for x in k_t]
        mb = [_mm3_parts(*ar[p], blockdiag(bsp[p][0]), blockdiag(bsp[p][1]), "nt") for p in pairs]
        mk = [_mm3_parts(*ar[p], blockdiag(ksp[p][0]), blockdiag(ksp[p][1]), "nt") for p in pairs]
        low = [jnp.where(strict, x[:C], 0.0) for x in mb]
        ak = [jnp.where(strict, x[:C], 0.0) for x in mk]
        rbk = [_cat([jnp.where(lower, mb[p][C:], 0.0), jnp.where(lower, mk[p][C:], 0.0)], 1)
               for p in pairs]

        ld = [jnp.where(same_block[8], x, 0.0) for x in low]
        l2 = prod(ld, ld)
        l4 = prod(l2, l2)
        l3 = prod(ld, l2)
        p1 = [eye + ld[p] + l2[p] + l3[p] for p in pairs]
        p1l4 = prod(p1, l4)
        t = [p1[p] + p1l4[p] for p in pairs]
        b = 8
        while b < C:
            couple = jnp.logical_not(same_block[b])
            if 2 * b < C:
                couple = jnp.logical_and(same_block[2 * b], couple)
            x = [jnp.where(couple, y, 0.0) for y in low]
            tx = prod(t, x)
            txt = prod(tx, t)
            t = [t[p] + txt[p] for p in pairs]
            b *= 2

        akv = prod(ak, v)
        zs = [_split2(x) for x in z]
        st = [_mm3_parts(*ar[p], *zs[p], "nn") for p in pairs]
        rhs = [st[p][:C] + akv[p] for p in pairs]
        u = prod(t, rhs)
        usp = [_split2(x) for x in u]
        vsp = [_split2(x) for x in v]
        uvh = [_cat([blockdiag(usp[p][0]), blockdiag(vsp[p][0])], 0) for p in pairs]
        uvl = [_cat([blockdiag(usp[p][1]), blockdiag(vsp[p][1])], 0) for p in pairs]
        rsp = [_split2(x) for x in rbk]
        y = [st[p][C:] + _mm3_parts(*rsp[p], uvh[p], uvl[p], "nn") for p in pairs]

        upd = []
        for p in pairs:
            scale = decay_in[p][C - 1:C, :]
            bkh, bkl = _split2(_cat([b_t[p] * scale, k_t[p] * scale], 0))
            upd.append(_mm3_parts(bkh, bkl, _cat([usp[p][0], vsp[p][0]], 0),
                                  _cat([usp[p][1], vsp[p][1]], 0), "tn"))
        decay_cols = [jnp.exp(_dot_tn(x, ones3)) for x in lw3]
        z = [decay_cols[p] * z[p] + jnp.where(head_bd, upd[p], 0.0) for p in pairs]

        mu = [_mm2(x, avg_bd2) for x in y]
        d = [y[p] - mu[p] for p in pairs]
        var = [_mm2(x * x, avg_bd2) for x in d]
        for p in pairs:
            yn = d[p] * lax.rsqrt(var[p] + LNX_EPS) * lng_ref[p] + lnb_ref[p]
            o_ref[0, p, rows, :] = (yn + bonus_ref[0, p, rows, :]).astype(o_ref.dtype)
    for p in pairs:
        z_ref[p] = z[p]


def _rwkv_scan(r, k, v, lw, kk, bb, bonus, lnx_g, lnx_b, chunks_per_step):
    batch, _, seq, _ = r.shape
    tc = CHUNK * chunks_per_step
    blk = pl.BlockSpec((1, N_PAIRS, tc, LANES), lambda b, c: (b, 0, c, 0))
    vec = _const_spec((N_PAIRS, 1, LANES))
    return pl.pallas_call(
        functools.partial(_rwkv_scan_kernel, chunks_per_step=chunks_per_step),
        grid=(batch, seq // tc),
        in_specs=[blk] * 7 + [vec, vec],
        out_specs=blk,
        out_shape=jax.ShapeDtypeStruct(r.shape, BF16),
        scratch_shapes=[pltpu.VMEM((N_PAIRS, LANES, LANES), F32)],
        compiler_params=_params(2),
        name="rwkv_scan",
    )(r, k, v, lw, kk, bb, bonus, lnx_g, lnx_b)


def _hi_lo(w):
    hi = w.astype(BF16)
    lo = (w - hi.astype(F32)).astype(BF16)
    return hi, lo


def kernel(x, moba_norm_g, moba_w_in, moba_w_out, rwkv_norm_g, rwkv_mix, rwkv_w_in, rwkv_w0, rwkv_w1, rwkv_w2, rwkv_a0, rwkv_a1, rwkv_a2, rwkv_k_k, rwkv_k_a, rwkv_r_k, rwkv_lnx_g, rwkv_lnx_b, rwkv_w_out, final_norm_g):
    batch, seq, d = x.shape
    assert d == D_MODEL and seq % MOBA_BLOCK == 0 and seq // MOBA_BLOCK <= 32
    x2d = x.reshape(batch * seq, d)
    row = lambda t: t.reshape(1, -1).astype(F32)

    w_hi, w_lo = _hi_lo(moba_w_in)
    q, k, v, sg = _moba_proj(x2d, row(moba_norm_g), w_hi, w_lo[:, :2 * WIDTH], batch, seq, tm=512)
    slopes = jnp.asarray([2.0 ** (-8.0 * (i + 1) / N_HEADS) for i in range(N_HEADS)], F32)
    slope_hi = (slopes * LOG2E).astype(BF16).astype(F32)
    sfeat = jnp.stack([slope_hi, slopes * LOG2E - slope_hi], axis=1).reshape(-1)
    o = _moba_attn(sfeat, slopes, q, k, v, unroll=2)
    x1 = _out_proj(o, sg, x2d, moba_w_out.astype(BF16), row(final_norm_g), seq, tm=512,
                   final_norm=False)

    w1h, w1l = _hi_lo(rwkv_w1)
    w2h, w2l = _hi_lo(rwkv_w2)
    a1h, a1l = _hi_lo(rwkv_a1)
    a2h, a2l = _hi_lo(rwkv_a2)
    weights = (row(rwkv_norm_g), rwkv_mix.astype(F32), rwkv_w_in.astype(BF16),
               row(rwkv_w0), w1h, w1l, w2h, w2l, row(rwkv_a0), a1h, a1l, a2h, a2l,
               row(rwkv_k_k), row(rwkv_k_a), row(rwkv_r_k))
    r, k2, v2, lw, kk, bb, bonus, sg2 = _rwkv_proj(x1, weights, batch, seq, tm=256)
    pair_vec = lambda t: t.reshape(N_PAIRS, 1, LANES).astype(F32)
    o2 = _rwkv_scan(r, k2, v2, lw, kk, bb, bonus, pair_vec(rwkv_lnx_g), pair_vec(rwkv_lnx_b),
                    chunks_per_step=1)
    out = _out_proj(o2, sg2, x1, rwkv_w_out.astype(BF16), row(final_norm_g), seq, tm=512,
                    final_norm=True)
    return out.reshape(batch, seq, d)
```

```python
import functools
import math

import jax
import jax.numpy as jnp
from jax import lax
from jax.experimental import pallas as pl
from jax.experimental.pallas import tpu as pltpu

F32 = jnp.float32
BF16 = jnp.bfloat16

D_MODEL = 1024
N_HEADS = 16
HEAD_DIM = 64
WIDTH = N_HEADS * HEAD_DIM
LANES = 128
N_PAIRS = WIDTH // LANES
MOBA_BLOCK = 256
MOBA_TOPK = 3
BF16_SUBLANES = 16
V_ROWS = HEAD_DIM + BF16_SUBLANES
LORA = 64
NORM_EPS = 1e-6
LNX_EPS = 64e-5
CHUNK = 64
VMEM_LIMIT = 56 * 1024 * 1024
NEG_INF = float("-inf")
POS_INF = float("inf")
LOG2E = math.log2(math.e)


def _dot(a, b):
    return jnp.dot(a, b, preferred_element_type=F32)


def _dot_nt(a, b):
    return lax.dot_general(a, b, (((1,), (1,)), ((), ())), preferred_element_type=F32)


def _dot_tn(a, b):
    return lax.dot_general(a, b, (((0,), (0,)), ((), ())), preferred_element_type=F32)


def _split2(x):
    hi = x.astype(BF16)
    lo = (x - hi.astype(F32)).astype(BF16)
    return hi, lo


def _split3(x):
    hi = x.astype(BF16)
    r1 = x - hi.astype(F32)
    mid = r1.astype(BF16)
    lo = (r1 - mid.astype(F32)).astype(BF16)
    return hi, mid, lo


def _cat(xs, axis):
    return jnp.concatenate(xs, axis=axis)


def _mm3_parts(ah, al, bh, bl, kind):
    if kind == "nn":
        return _dot(_cat([ah, al], 1), _cat([bh, bh], 0)) + _dot(ah, bl)
    if kind == "nt":
        return _dot_nt(_cat([ah, al], 1), _cat([bh, bh], 1)) + _dot_nt(ah, bl)
    assert kind == "tn"
    return _dot_tn(_cat([ah, al], 0), _cat([bh, bh], 0)) + _dot_tn(ah, bl)


def _mm_left(ah, al, b, kind):
    if kind == "nn":
        return _dot(_cat([ah, al], 1), _cat([b, b], 0))
    if kind == "nt":
        return _dot_nt(_cat([ah, al], 1), _cat([b, b], 1))
    assert kind == "tn"
    return _dot_tn(_cat([ah, al], 0), _cat([b, b], 0))


def _mm3(a, b, kind="nn"):
    ah, al = _split2(a)
    bh, bl = _split2(b)
    return _mm3_parts(ah, al, bh, bl, kind)


def _mm2(a, b2_exact):
    ah, al = _split2(a)
    return _dot(_cat([ah, al], 1), b2_exact)


def _rms(x, g):
    ms = jnp.mean(x * x, axis=-1, keepdims=True)
    return x * lax.rsqrt(ms + NORM_EPS) * g


def _iota(shape, dim):
    return lax.broadcasted_iota(jnp.int32, shape, dim)


def _head_blockdiag(n):
    return (_iota((n, n), 0) // HEAD_DIM) == (_iota((n, n), 1) // HEAD_DIM)


def _const_spec(shape):
    nd = len(shape)
    return pl.BlockSpec(shape, lambda *_: (0,) * nd)


def _params(n_axes):
    return pltpu.CompilerParams(dimension_semantics=("arbitrary",) * n_axes,
                                vmem_limit_bytes=VMEM_LIMIT)


def _moba_proj_kernel(x_ref, g_ref, w_ref, q_ref, k_ref, v_ref, sg_ref):
    h = _rms(x_ref[...], g_ref[...]).astype(BF16)
    q = _dot(h, w_ref[:, 0 * WIDTH:1 * WIDTH])
    k = _dot(h, w_ref[:, 1 * WIDTH:2 * WIDTH])
    v = _dot(h, w_ref[:, 2 * WIDTH:3 * WIDTH])
    gate = _dot(h, w_ref[:, 3 * WIDTH:4 * WIDTH])
    for p in range(N_PAIRS):
        cols = slice(p * LANES, (p + 1) * LANES)
        q_ref[0, p] = q[:, cols]
        k_ref[0, p] = k[:, cols].astype(BF16)
        v_ref[0, p] = v[:, cols].astype(BF16)
    sg_ref[...] = (gate * jax.nn.sigmoid(gate)).astype(BF16)


def _moba_proj(x2d, norm_g, w_bf16, batch, seq, tm):
    m = x2d.shape[0]
    nsb = seq // tm
    pair_spec = pl.BlockSpec((1, N_PAIRS, tm, LANES), lambda i: (i // nsb, 0, i % nsb, 0))
    pair_shape = (batch, N_PAIRS, seq, LANES)
    return pl.pallas_call(
        _moba_proj_kernel,
        grid=(m // tm,),
        in_specs=[
            pl.BlockSpec((tm, D_MODEL), lambda i: (i, 0)),
            _const_spec((1, D_MODEL)),
            _const_spec((D_MODEL, 4 * WIDTH)),
        ],
        out_specs=[pair_spec, pair_spec, pair_spec,
                   pl.BlockSpec((tm, WIDTH), lambda i: (i, 0))],
        out_shape=[
            jax.ShapeDtypeStruct(pair_shape, F32),
            jax.ShapeDtypeStruct(pair_shape, BF16),
            jax.ShapeDtypeStruct(pair_shape, BF16),
            jax.ShapeDtypeStruct((m, WIDTH), BF16),
        ],
        compiler_params=_params(1),
        name="moba_proj",
    )(x2d, norm_g, w_bf16)


def _moba_attn_kernel(sfeat_ref, slopes_ref, q_ref, k_ref, v_ref, o_ref,
                      kmean_ref, kaug_ref, vt_ref, s_ref, acc_ref, *, nb, unroll):
    L = MOBA_BLOCK
    p = pl.program_id(1)
    n = pl.program_id(2)
    lane = _iota((L, LANES), 1)
    head_lo = lane < HEAD_DIM
    head_masks = (head_lo, jnp.logical_not(head_lo))
    feat_base = (HEAD_DIM, 0)

    @pl.when(n == 0)
    def _():
        offs = _iota((L, LANES), 0).astype(F32)
        ones_row = (_iota((V_ROWS - HEAD_DIM, L), 0) == 0).astype(BF16)
        for j in range(nb):
            rows = slice(j * L, (j + 1) * L)
            kf = k_ref[0, 0, rows, :].astype(F32)
            kmean_ref[j:j + 1, :] = jnp.mean(kf, axis=0, keepdims=True)
            for hh in range(2):
                fb = feat_base[hh]
                feat = jnp.where(jnp.logical_or(lane == fb, lane == fb + 1), offs, 0.0)
                kaug_ref[hh, rows, :] = jnp.where(head_masks[hh], kf, feat).astype(BF16)
            v_t = v_ref[0, 0, rows, :].astype(F32).T.astype(BF16)
            for hh in range(2):
                vt_ref[j, hh, :HEAD_DIM, :] = v_t[hh * HEAD_DIM:(hh + 1) * HEAD_DIM, :]
                vt_ref[j, hh, HEAD_DIM:, :] = ones_row

    qf = q_ref[0, 0]
    qs = qf * (HEAD_DIM ** -0.5 * LOG2E)
    blk = _iota((nb, L), 0)
    kmean = kmean_ref[...]
    key_i = _iota((L, L), 0)
    qry_i = _iota((L, L), 1)
    start = pl.multiple_of(n * L, L)

    qa, bits = [], []
    for hh in range(2):
        h = 2 * p + hh
        gate = _mm3(kmean, jnp.where(head_masks[hh], qf, 0.0), "nt")
        gate = jnp.where(blk < n, gate, NEG_INF)
        chosen_bits = jnp.zeros((1, L), jnp.int32)
        for _ in range(MOBA_TOPK):
            mx = jnp.max(gate, axis=0, keepdims=True)
            first = jnp.min(jnp.where(gate == mx, blk, nb), axis=0, keepdims=True)
            valid = mx > NEG_INF
            chosen_bits = chosen_bits | jnp.where(valid, jnp.left_shift(1, first), 0)
            gate = jnp.where(jnp.logical_and(blk == first, valid), NEG_INF, gate)
        bits.append(chosen_bits)
        fb = feat_base[hh]
        feat = jnp.where(lane == fb, sfeat_ref[2 * h],
                         jnp.where(lane == fb + 1, sfeat_ref[2 * h + 1], 0.0))
        qa.append(jnp.where(head_masks[hh], qs, feat).astype(BF16))

    def scores(j):
        off = pl.multiple_of(j * L, L)
        return [_dot_nt(kaug_ref[hh, pl.ds(off, L), :], qa[hh]) for hh in range(2)]

    def issue_scores(g, slot):
        maxes = []
        for u in range(unroll):
            for hh, s in enumerate(scores(jnp.minimum(g * unroll + u, nb - 1))):
                s_ref[slot, 2 * u + hh] = s
                maxes.append(jnp.max(s, axis=0, keepdims=True))
        return maxes

    def consume(g, slot, maxes, ms):
        ms = list(ms)
        for u in range(unroll):
            j = g * unroll + u
            jc = jnp.minimum(j, nb - 1)
            in_range = j < n
            dist = ((j - n) * L).astype(F32) * LOG2E
            for hh in range(2):
                c = dist * slopes_ref[2 * p + hh]
                chosen = jnp.logical_and((jnp.right_shift(bits[hh], jc) & 1) == 1, in_range)
                m_new = jnp.maximum(ms[hh], jnp.where(chosen, maxes[2 * u + hh] + c, NEG_INF))
                alpha = jnp.exp2(ms[hh] - m_new)
                shift = jnp.where(chosen, m_new - c, POS_INF)
                pr = jnp.exp2(s_ref[slot, 2 * u + hh] - shift).astype(BF16)
                acc_ref[hh] = alpha * acc_ref[hh] + _dot(vt_ref[jc, hh], pr)
                ms[hh] = m_new
        return ms

    own = scores(n)
    first_maxes = issue_scores(0, 0)
    ms = []
    for hh in range(2):
        s = jnp.where(key_i <= qry_i, own[hh], NEG_INF)
        m = jnp.max(s, axis=0, keepdims=True)
        pr = jnp.exp2(s - m).astype(BF16)
        acc_ref[hh] = _dot(vt_ref[n, hh], pr)
        ms.append(m)

    nm = 2 * unroll

    def body(i, carry):
        maxes, ms = carry[:nm], carry[nm:]
        maxes_b = issue_scores(2 * i + 1, 1)
        ms = consume(2 * i, 0, maxes, ms)
        maxes_a = issue_scores(2 * i + 2, 0)
        ms = consume(2 * i + 1, 1, maxes_b, ms)
        return (*maxes_a, *ms)

    per_trip = 2 * unroll
    lax.fori_loop(0, (n + per_trip - 1) // per_trip, body, (*first_maxes, *ms))
    acc0, acc1 = acc_ref[0], acc_ref[1]
    o_t = _cat([acc0[:HEAD_DIM] / acc0[HEAD_DIM:HEAD_DIM + 1],
                acc1[:HEAD_DIM] / acc1[HEAD_DIM:HEAD_DIM + 1]], 0)
    o_ref[0, 0] = o_t.T.astype(o_ref.dtype)


def _moba_attn(sfeat, slopes, q, k, v, unroll):
    batch, _, seq, _ = q.shape
    nb = seq // MOBA_BLOCK
    tile = pl.BlockSpec((1, 1, MOBA_BLOCK, LANES), lambda b, p, n: (b, p, n, 0))
    full = pl.BlockSpec((1, 1, seq, LANES), lambda b, p, n: (b, p, 0, 0))
    smem = pl.BlockSpec(memory_space=pltpu.SMEM)
    return pl.pallas_call(
        functools.partial(_moba_attn_kernel, nb=nb, unroll=unroll),
        grid=(batch, N_PAIRS, nb),
        in_specs=[smem, smem, tile, full, full],
        out_specs=tile,
        out_shape=jax.ShapeDtypeStruct(q.shape, BF16),
        scratch_shapes=[pltpu.VMEM((nb, LANES), F32),
                        pltpu.VMEM((2, seq, LANES), BF16),
                        pltpu.VMEM((nb, 2, V_ROWS, MOBA_BLOCK), BF16),
                        pltpu.VMEM((2, 2 * unroll, MOBA_BLOCK, MOBA_BLOCK), F32),
                        pltpu.VMEM((2, V_ROWS, MOBA_BLOCK), F32)],
        compiler_params=_params(3),
        name="moba_attn",
    )(sfeat, slopes, q, k, v)


def _out_proj_kernel(o_ref, sg_ref, x_ref, w_ref, g_ref, y_ref, *, final_norm):
    o = _cat([o_ref[0, p] for p in range(N_PAIRS)], 1)
    gated = (o.astype(F32) * sg_ref[...].astype(F32)).astype(BF16)
    y = x_ref[...] + _dot(gated, w_ref[...])
    if final_norm:
        y = _rms(y, g_ref[...])
    y_ref[...] = y


def _out_proj(o_pairs, sg, x2d, w_bf16, norm_g, seq, tm, final_norm):
    m = x2d.shape[0]
    nsb = seq // tm
    row_spec = pl.BlockSpec((tm, D_MODEL), lambda i: (i, 0))
    return pl.pallas_call(
        functools.partial(_out_proj_kernel, final_norm=final_norm),
        grid=(m // tm,),
        in_specs=[
            pl.BlockSpec((1, N_PAIRS, tm, LANES), lambda i: (i // nsb, 0, i % nsb, 0)),
            row_spec, row_spec,
            _const_spec((WIDTH, D_MODEL)),
            _const_spec((1, D_MODEL)),
        ],
        out_specs=row_spec,
        out_shape=jax.ShapeDtypeStruct((m, D_MODEL), F32),
        compiler_params=_params(1),
        name="out_proj_final" if final_norm else "out_proj",
    )(o_pairs, sg, x2d, w_bf16, norm_g)


def _rwkv_proj_kernel(x_ref, halo_ref, g_ref, mix_ref, win_ref, w0_ref, w1h_ref, w1l_ref,
                      w2h_ref, w2l_ref, a0_ref, a1h_ref, a1l_ref, a2h_ref, a2l_ref,
                      kk_ref, ka_ref, rk_ref,
                      r_out, k_out, v_out, lw_out, kk_out, bb_out, bonus_out, sg_out,
                      *, tiles_per_seq):
    i = pl.program_id(0)
    tm = x_ref.shape[0]
    g = g_ref[...]
    h = _rms(x_ref[...], g)
    prev_row = _rms(halo_ref[...], g)[7:8, :]
    prev_row = jnp.where(i % tiles_per_seq == 0, 0.0, prev_row)
    rolled = pltpu.roll(h, 1, 0)
    h_prev = jnp.where(_iota((tm, D_MODEL), 0) == 0, prev_row, rolled)
    xx = h_prev - h

    def stream(n):
        return h + xx * mix_ref[n:n + 1, :]

    r = _dot(stream(0).astype(BF16), win_ref[0])
    k = _dot(stream(1).astype(BF16), win_ref[1])
    v = _dot(stream(2).astype(BF16), win_ref[2])
    gt = _dot(stream(3).astype(BF16), win_ref[3])
    sg_out[...] = (gt * jax.nn.sigmoid(gt)).astype(BF16)

    def lora(xs, ah_ref, al_ref, bh_ref, bl_ref, act):
        xh, xl = _split2(xs)
        mid = _dot(xh, ah_ref[...]) + _dot(xh, al_ref[...]) + _dot(xl, ah_ref[...])
        mid = act(mid)
        mh, ml = _split2(mid)
        return _dot(mh, bh_ref[...]) + _dot(mh, bl_ref[...]) + _dot(ml, bh_ref[...])

    z = -(w0_ref[...] + lora(stream(4), w1h_ref, w1l_ref, w2h_ref, w2l_ref, jnp.tanh))
    softplus = jnp.maximum(z, 0.0) + jnp.log(1.0 + jnp.exp(-jnp.abs(z)))
    lw = -jnp.exp(-softplus - 0.5)
    a = jax.nn.sigmoid(a0_ref[...] + lora(stream(5), a1h_ref, a1l_ref, a2h_ref, a2l_ref,
                                          lambda t: t))
    kr = k * kk_ref[...]
    k_mod = k * (1.0 + (a - 1.0) * ka_ref[...])
    rk = r * k_mod * rk_ref[...]
    ones_bd = _head_blockdiag(LANES).astype(BF16)
    ones_bd2 = _cat([ones_bd, ones_bd], 0)
    for p in range(N_PAIRS):
        cols = slice(p * LANES, (p + 1) * LANES)
        kr_p = kr[:, cols]
        ss = _mm2(kr_p * kr_p, ones_bd2)
        kk_p = kr_p / jnp.maximum(jnp.sqrt(ss), 1e-12)
        r_out[0, p] = r[:, cols]
        k_out[0, p] = k_mod[:, cols]
        v_out[0, p] = v[:, cols]
        lw_out[0, p] = lw[:, cols]
        kk_out[0, p] = kk_p
        bb_out[0, p] = kk_p * a[:, cols]
        bonus_out[0, p] = _mm2(rk[:, cols], ones_bd2) * v[:, cols]


def _rwkv_proj(x2d, weights, batch, seq, tm):
    m = x2d.shape[0]
    nsb = seq // tm
    halo_blocks = tm // 8
    pair_spec = pl.BlockSpec((1, N_PAIRS, tm, LANES), lambda i: (i // nsb, 0, i % nsb, 0))
    pair_shape = jax.ShapeDtypeStruct((batch, N_PAIRS, seq, LANES), F32)
    vec = _const_spec((1, WIDTH))
    lora_in = _const_spec((D_MODEL, LORA))
    lora_out = _const_spec((LORA, WIDTH))
    return pl.pallas_call(
        functools.partial(_rwkv_proj_kernel, tiles_per_seq=nsb),
        grid=(m // tm,),
        in_specs=[
            pl.BlockSpec((tm, D_MODEL), lambda i: (i, 0)),
            pl.BlockSpec((8, D_MODEL), lambda i: (jnp.maximum(i * halo_blocks - 1, 0), 0)),
            _const_spec((1, D_MODEL)),
            _const_spec((6, D_MODEL)),
            _const_spec((4, D_MODEL, WIDTH)),
            vec, lora_in, lora_in, lora_out, lora_out,
            vec, lora_in, lora_in, lora_out, lora_out,
            vec, vec, vec,
        ],
        out_specs=[pair_spec] * 7 + [pl.BlockSpec((tm, WIDTH), lambda i: (i, 0))],
        out_shape=[pair_shape] * 7 + [jax.ShapeDtypeStruct((m, WIDTH), BF16)],
        compiler_params=_params(1),
        name="rwkv_proj",
    )(x2d, x2d, *weights)


def _rwkv_scan_kernel(r_ref, k_ref, v_ref, lw_ref, kk_ref, bb_ref, bonus_ref, lng_ref, lnb_ref,
                      o_ref, z_ref, *, chunks_per_step):
    C = CHUNK
    pairs = range(N_PAIRS)

    @pl.when(pl.program_id(1) == 0)
    def _():
        z_ref[...] = jnp.zeros_like(z_ref)

    row = _iota((C, LANES), 0)
    lane = _iota((C, LANES), 1)
    col = lane % C
    head_lo = lane < HEAD_DIM
    keep_lo = head_lo.astype(BF16)
    keep_hi = jnp.logical_not(head_lo).astype(BF16)
    eye = (row == col).astype(F32)
    strict = col < row
    lower2 = (_iota((C, 2 * LANES), 1) % C) <= _iota((C, 2 * LANES), 0)
    same_block = {b: (row // b) == (col // b) for b in (8, 16, 32)}
    tri3 = ((_iota((C, 3 * C), 1) % C) <= _iota((C, 3 * C), 0)).astype(BF16)
    head_bd = _head_blockdiag(LANES)
    avg_bd = head_bd.astype(BF16) * (1.0 / HEAD_DIM)
    avg_bd2 = _cat([avg_bd, avg_bd], 0)

    def blockdiag(xb):
        return _cat([xb * keep_lo, xb * keep_hi], 0)

    def prod(a_list, b_list):
        parts = [(_split2(a), b.astype(BF16)) for a, b in zip(a_list, b_list)]
        return [_mm_left(ah, al, blockdiag(bb), "nn") for (ah, al), bb in parts]

    z = [z_ref[p] for p in pairs]
    for c in range(chunks_per_step):
        rows = slice(c * C, (c + 1) * C)
        v = [v_ref[0, p, rows, :] for p in pairs]
        lw = [lw_ref[0, p, rows, :] for p in pairs]
        lw3 = [_cat(_split3(x), 0) for x in lw]
        cs = [_dot(tri3, x) for x in lw3]
        decay_in = [jnp.exp(x) for x in cs]
        inv = [jnp.exp(-x) for x in cs]
        r_t = [r_ref[0, p, rows, :] * decay_in[p] for p in pairs]
        k_t = [k_ref[0, p, rows, :] * inv[p] for p in pairs]
        b_t = [bb_ref[0, p, rows, :] * inv[p] for p in pairs]
        a_t = [-(kk_ref[0, p, rows, :] * jnp.exp(cs[p] - lw[p])) for p in pairs]
        ar = [_split2(_cat([a_t[p], r_t[p]], 0)) for p in pairs]
        bk_rows = [_cat([blockdiag(b_t[p].astype(BF16)), blockdiag(k_t[p].astype(BF16))], 0)
                   for p in pairs]
        mbk = [_mm_left(*ar[p], bk_rows[p], "nt") for p in pairs]
        low = [jnp.where(strict, x[:C, :LANES], 0.0) for x in mbk]
        ak = [jnp.where(strict, x[:C, LANES:], 0.0) for x in mbk]
        rbk = [jnp.where(lower2, x[C:], 0.0) for x in mbk]

        ld = [jnp.where(same_block[8], x, 0.0) for x in low]
        l2 = prod(ld, ld)
        l4 = prod(l2, l2)
        l3 = prod(ld, l2)
        p1 = [eye + ld[p] + l2[p] + l3[p] for p in pairs]
        p1l4 = prod(p1, l4)
        t = [p1[p] + p1l4[p] for p in pairs]
        b = 8
        while b < C:
            couple = jnp.logical_not(same_block[b])
            if 2 * b < C:
                couple = jnp.logical_and(same_block[2 * b], couple)
            x = [jnp.where(couple, y, 0.0) for y in low]
            tx = prod(t, x)
            txt = prod(tx, t)
            t = [t[p] + txt[p] for p in pairs]
            b *= 2

        akv = prod(ak, v)
        st = [_mm_left(*ar[p], z[p].astype(BF16), "nn") for p in pairs]
        rhs = [st[p][:C] + akv[p] for p in pairs]
        u = prod(t, rhs)
        ub = [x.astype(BF16) for x in u]
        vb = [x.astype(BF16) for x in v]
        uv = [_cat([blockdiag(ub[p]), blockdiag(vb[p])], 0) for p in pairs]
        rsp = [_split2(x) for x in rbk]
        y = [st[p][C:] + _mm_left(*rsp[p], uv[p], "nn") for p in pairs]

        upd = []
        for p in pairs:
            scale = decay_in[p][C - 1:C, :]
            bkh, bkl = _split2(_cat([b_t[p] * scale, k_t[p] * scale], 0))
            upd.append(_mm_left(bkh, bkl, _cat([ub[p], vb[p]], 0), "tn"))
        decay_cols = [x[C - 8:C, :].T[:, 7:8] for x in decay_in]
        z = [decay_cols[p] * z[p] + jnp.where(head_bd, upd[p], 0.0) for p in pairs]

        mu = [_mm2(x, avg_bd2) for x in y]
        d = [y[p] - mu[p] for p in pairs]
        var = [_mm2(x * x, avg_bd2) for x in d]
        for p in pairs:
            yn = d[p] * lax.rsqrt(var[p] + LNX_EPS) * lng_ref[p] + lnb_ref[p]
            o_ref[0, p, rows, :] = (yn + bonus_ref[0, p, rows, :]).astype(o_ref.dtype)
    for p in pairs:
        z_ref[p] = z[p]


def _rwkv_scan(r, k, v, lw, kk, bb, bonus, lnx_g, lnx_b, chunks_per_step):
    batch, _, seq, _ = r.shape
    tc = CHUNK * chunks_per_step
    blk = pl.BlockSpec((1, N_PAIRS, tc, LANES), lambda b, c: (b, 0, c, 0))
    vec = _const_spec((N_PAIRS, 1, LANES))
    return pl.pallas_call(
        functools.partial(_rwkv_scan_kernel, chunks_per_step=chunks_per_step),
        grid=(batch, seq // tc),
        in_specs=[blk] * 7 + [vec, vec],
        out_specs=blk,
        out_shape=jax.ShapeDtypeStruct(r.shape, BF16),
        scratch_shapes=[pltpu.VMEM((N_PAIRS, LANES, LANES), F32)],
        compiler_params=_params(2),
        name="rwkv_scan",
    )(r, k, v, lw, kk, bb, bonus, lnx_g, lnx_b)


def _hi_lo(w):
    hi = w.astype(BF16)
    lo = (w - hi.astype(F32)).astype(BF16)
    return hi, lo


def kernel(x, moba_norm_g, moba_w_in, moba_w_out, rwkv_norm_g, rwkv_mix, rwkv_w_in, rwkv_w0, rwkv_w1, rwkv_w2, rwkv_a0, rwkv_a1, rwkv_a2, rwkv_k_k, rwkv_k_a, rwkv_r_k, rwkv_lnx_g, rwkv_lnx_b, rwkv_w_out, final_norm_g):
    batch, seq, d = x.shape
    assert d == D_MODEL and seq % MOBA_BLOCK == 0 and seq // MOBA_BLOCK <= 32
    x2d = x.reshape(batch * seq, d)
    row = lambda t: t.reshape(1, -1).astype(F32)

    q, k, v, sg = _moba_proj(x2d, row(moba_norm_g), moba_w_in.astype(BF16), batch, seq, tm=512)
    slopes = jnp.asarray([2.0 ** (-8.0 * (i + 1) / N_HEADS) for i in range(N_HEADS)], F32)
    slope_hi = (slopes * LOG2E).astype(BF16).astype(F32)
    sfeat = jnp.stack([slope_hi, slopes * LOG2E - slope_hi], axis=1).reshape(-1)
    o = _moba_attn(sfeat, slopes, q, k, v, unroll=2)
    x1 = _out_proj(o, sg, x2d, moba_w_out.astype(BF16), row(final_norm_g), seq, tm=512,
                   final_norm=False)

    w1h, w1l = _hi_lo(rwkv_w1)
    w2h, w2l = _hi_lo(rwkv_w2)
    a1h, a1l = _hi_lo(rwkv_a1)
    a2h, a2l = _hi_lo(rwkv_a2)
    weights = (row(rwkv_norm_g), rwkv_mix.astype(F32), rwkv_w_in.astype(BF16),
               row(rwkv_w0), w1h, w1l, w2h, w2l, row(rwkv_a0), a1h, a1l, a2h, a2l,
               row(rwkv_k_k), row(rwkv_k_a), row(rwkv_r_k))
    r, k2, v2, lw, kk, bb, bonus, sg2 = _rwkv_proj(x1, weights, batch, seq, tm=256)
    pair_vec = lambda t: t.reshape(N_PAIRS, 1, LANES).astype(F32)
    o2 = _rwkv_scan(r, k2, v2, lw, kk, bb, bonus, pair_vec(rwkv_lnx_g), pair_vec(rwkv_lnx_b),
                    chunks_per_step=1)
    out = _out_proj(o2, sg2, x1, rwkv_w_out.astype(BF16), row(final_norm_g), seq, tm=512,
                    final_norm=True)
    return out.reshape(batch, seq, d)
```

```python
import functools
import math

import jax
import jax.numpy as jnp
from jax import lax
from jax.experimental import pallas as pl
from jax.experimental.pallas import tpu as pltpu

F32 = jnp.float32
BF16 = jnp.bfloat16

D_MODEL = 1024
N_HEADS = 16
HEAD_DIM = 64
WIDTH = N_HEADS * HEAD_DIM
LANES = 128
N_PAIRS = WIDTH // LANES
MOBA_BLOCK = 256
MOBA_TOPK = 3
BF16_SUBLANES = 16
V_ROWS = HEAD_DIM + BF16_SUBLANES
LORA = 64
NORM_EPS = 1e-6
LNX_EPS = 64e-5
CHUNK = 64
SCAN_FREE_STAGES_PER_STEP_STAGE = 3
VMEM_LIMIT = 56 * 1024 * 1024
NEG_INF = float("-inf")
POS_INF = float("inf")
LOG2E = math.log2(math.e)


def _dot(a, b):
    return jnp.dot(a, b, preferred_element_type=F32)


def _dot_nt(a, b):
    return lax.dot_general(a, b, (((1,), (1,)), ((), ())), preferred_element_type=F32)


def _dot_tn(a, b):
    return lax.dot_general(a, b, (((0,), (0,)), ((), ())), preferred_element_type=F32)


def _split2(x):
    hi = x.astype(BF16)
    lo = (x - hi.astype(F32)).astype(BF16)
    return hi, lo


def _split3(x):
    hi = x.astype(BF16)
    r1 = x - hi.astype(F32)
    mid = r1.astype(BF16)
    lo = (r1 - mid.astype(F32)).astype(BF16)
    return hi, mid, lo


def _cat(xs, axis):
    return jnp.concatenate(xs, axis=axis)


def _mm3_parts(ah, al, bh, bl, kind):
    if kind == "nn":
        return _dot(_cat([ah, al], 1), _cat([bh, bh], 0)) + _dot(ah, bl)
    if kind == "nt":
        return _dot_nt(_cat([ah, al], 1), _cat([bh, bh], 1)) + _dot_nt(ah, bl)
    assert kind == "tn"
    return _dot_tn(_cat([ah, al], 0), _cat([bh, bh], 0)) + _dot_tn(ah, bl)


def _mm_left(ah, al, b, kind):
    if kind == "nn":
        return _dot(_cat([ah, al], 1), _cat([b, b], 0))
    if kind == "nt":
        return _dot_nt(_cat([ah, al], 1), _cat([b, b], 1))
    assert kind == "tn"
    return _dot_tn(_cat([ah, al], 0), _cat([b, b], 0))


def _mm3(a, b, kind="nn"):
    ah, al = _split2(a)
    bh, bl = _split2(b)
    return _mm3_parts(ah, al, bh, bl, kind)


def _mm2(a, b2_exact):
    ah, al = _split2(a)
    return _dot(_cat([ah, al], 1), b2_exact)


def _rms(x, g):
    ms = jnp.mean(x * x, axis=-1, keepdims=True)
    return x * lax.rsqrt(ms + NORM_EPS) * g


def _iota(shape, dim):
    return lax.broadcasted_iota(jnp.int32, shape, dim)


def _head_blockdiag(n):
    return (_iota((n, n), 0) // HEAD_DIM) == (_iota((n, n), 1) // HEAD_DIM)


def _const_spec(shape):
    nd = len(shape)
    return pl.BlockSpec(shape, lambda *_: (0,) * nd)


def _params(n_axes):
    return pltpu.CompilerParams(dimension_semantics=("arbitrary",) * n_axes,
                                vmem_limit_bytes=VMEM_LIMIT)


def _moba_proj_kernel(x_ref, g_ref, w_ref, q_ref, k_ref, v_ref, sg_ref):
    h = _rms(x_ref[...], g_ref[...]).astype(BF16)
    q = _dot(h, w_ref[:, 0 * WIDTH:1 * WIDTH])
    k = _dot(h, w_ref[:, 1 * WIDTH:2 * WIDTH])
    v = _dot(h, w_ref[:, 2 * WIDTH:3 * WIDTH])
    gate = _dot(h, w_ref[:, 3 * WIDTH:4 * WIDTH])
    for p in range(N_PAIRS):
        cols = slice(p * LANES, (p + 1) * LANES)
        q_ref[0, p] = q[:, cols]
        k_ref[0, p] = k[:, cols].astype(BF16)
        v_ref[0, p] = v[:, cols].astype(BF16)
    sg_ref[...] = (gate * jax.nn.sigmoid(gate)).astype(BF16)


def _moba_proj(x2d, norm_g, w_bf16, batch, seq, tm):
    m = x2d.shape[0]
    nsb = seq // tm
    pair_spec = pl.BlockSpec((1, N_PAIRS, tm, LANES), lambda i: (i // nsb, 0, i % nsb, 0))
    pair_shape = (batch, N_PAIRS, seq, LANES)
    return pl.pallas_call(
        _moba_proj_kernel,
        grid=(m // tm,),
        in_specs=[
            pl.BlockSpec((tm, D_MODEL), lambda i: (i, 0)),
            _const_spec((1, D_MODEL)),
            _const_spec((D_MODEL, 4 * WIDTH)),
        ],
        out_specs=[pair_spec, pair_spec, pair_spec,
                   pl.BlockSpec((tm, WIDTH), lambda i: (i, 0))],
        out_shape=[
            jax.ShapeDtypeStruct(pair_shape, F32),
            jax.ShapeDtypeStruct(pair_shape, BF16),
            jax.ShapeDtypeStruct(pair_shape, BF16),
            jax.ShapeDtypeStruct((m, WIDTH), BF16),
        ],
        compiler_params=_params(1),
        name="moba_proj",
    )(x2d, norm_g, w_bf16)


def _moba_attn_kernel(sfeat_ref, slopes_ref, q_ref, k_ref, v_ref, o_ref,
                      kmean_ref, kaug_ref, vt_ref, s_ref, acc_ref, *, nb, unroll, tiles):
    L = MOBA_BLOCK
    p = pl.program_id(1)
    n = pl.program_id(2)
    lane = _iota((L, LANES), 1)
    head_lo = lane < HEAD_DIM
    head_masks = (head_lo, jnp.logical_not(head_lo))
    feat_base = (HEAD_DIM, 0)

    @pl.when(n == 0)
    def _():
        offs = _iota((L, LANES), 0).astype(F32)
        ones_row = (_iota((V_ROWS - HEAD_DIM, L), 0) == 0).astype(BF16)
        for j in range(nb):
            rows = slice(j * L, (j + 1) * L)
            kf = k_ref[0, 0, rows, :].astype(F32)
            kmean_ref[j:j + 1, :] = jnp.mean(kf, axis=0, keepdims=True)
            for hh in range(2):
                fb = feat_base[hh]
                feat = jnp.where(jnp.logical_or(lane == fb, lane == fb + 1), offs, 0.0)
                kaug_ref[hh, rows, :] = jnp.where(head_masks[hh], kf, feat).astype(BF16)
            v_t = v_ref[0, 0, rows, :].astype(F32).T.astype(BF16)
            for hh in range(2):
                vt_ref[j, hh, :HEAD_DIM, :] = v_t[hh * HEAD_DIM:(hh + 1) * HEAD_DIM, :]
                vt_ref[j, hh, HEAD_DIM:, :] = ones_row

    n_first = n * tiles
    units = [(t, hh) for t in range(tiles) for hh in range(2)]
    blk = _iota((nb, L), 0)
    kmean = kmean_ref[...]
    key_i = _iota((L, L), 0)
    qry_i = _iota((L, L), 1)

    qa, bits = [], []
    for t, hh in units:
        h = 2 * p + hh
        qf = q_ref[0, 0, t * L:(t + 1) * L, :]
        gate = _mm3(kmean, jnp.where(head_masks[hh], qf, 0.0), "nt")
        gate = jnp.where(blk < n_first + t, gate, NEG_INF)
        chosen_bits = jnp.zeros((1, L), jnp.int32)
        for _ in range(MOBA_TOPK):
            mx = jnp.max(gate, axis=0, keepdims=True)
            first = jnp.min(jnp.where(gate == mx, blk, nb), axis=0, keepdims=True)
            valid = mx > NEG_INF
            chosen_bits = chosen_bits | jnp.where(valid, jnp.left_shift(1, first), 0)
            gate = jnp.where(jnp.logical_and(blk == first, valid), NEG_INF, gate)
        bits.append(chosen_bits)
        fb = feat_base[hh]
        feat = jnp.where(lane == fb, sfeat_ref[2 * h],
                         jnp.where(lane == fb + 1, sfeat_ref[2 * h + 1], 0.0))
        qa.append(jnp.where(head_masks[hh], qf * (HEAD_DIM ** -0.5 * LOG2E), feat).astype(BF16))
    qa_all = [_cat([qa[2 * t + hh] for t in range(tiles)], 0) for hh in range(2)]

    def scores(j):
        off = pl.multiple_of(j * L, L)
        per_head = [_dot_nt(kaug_ref[hh, pl.ds(off, L), :], qa_all[hh]) for hh in range(2)]
        return [per_head[hh][:, t * L:(t + 1) * L] for t, hh in units]

    nu = len(units)

    def issue_scores(g, slot):
        maxes = []
        for u in range(unroll):
            for w, s in enumerate(scores(jnp.minimum(g * unroll + u, nb - 1))):
                s_ref[slot, nu * u + w] = s
                maxes.append(jnp.max(s, axis=0, keepdims=True))
        return maxes

    def consume(g, slot, maxes, ms):
        ms = list(ms)
        for u in range(unroll):
            j = g * unroll + u
            jc = jnp.minimum(j, nb - 1)
            for w, (t, hh) in enumerate(units):
                tile = n_first + t
                c = ((j - tile) * L).astype(F32) * LOG2E * slopes_ref[2 * p + hh]
                chosen = jnp.logical_and((jnp.right_shift(bits[w], jc) & 1) == 1, j < tile)
                m_new = jnp.maximum(ms[w], jnp.where(chosen, maxes[nu * u + w] + c, NEG_INF))
                alpha = jnp.exp2(ms[w] - m_new)
                shift = jnp.where(chosen, m_new - c, POS_INF)
                pr = jnp.exp2(s_ref[slot, nu * u + w] - shift).astype(BF16)
                acc_ref[w] = alpha * acc_ref[w] + _dot(vt_ref[jc, hh], pr)
                ms[w] = m_new
        return ms

    own = [_dot_nt(kaug_ref[hh, pl.ds(pl.multiple_of((n_first + t) * L, L), L), :], qa[2 * t + hh])
           for t, hh in units]
    first_maxes = issue_scores(0, 0)
    ms = []
    for w, (t, hh) in enumerate(units):
        tile = n_first + t
        s = jnp.where(key_i <= qry_i, own[w], NEG_INF)
        m = jnp.max(s, axis=0, keepdims=True)
        pr = jnp.exp2(s - m).astype(BF16)
        acc_ref[w] = _dot(vt_ref[tile, hh], pr)
        ms.append(m)

    nm = nu * unroll

    def body(i, carry):
        maxes, ms = carry[:nm], carry[nm:]
        maxes_b = issue_scores(2 * i + 1, 1)
        ms = consume(2 * i, 0, maxes, ms)
        maxes_a = issue_scores(2 * i + 2, 0)
        ms = consume(2 * i + 1, 1, maxes_b, ms)
        return (*maxes_a, *ms)

    per_trip = 2 * unroll
    n_past = n_first + tiles - 1
    lax.fori_loop(0, (n_past + per_trip - 1) // per_trip, body, (*first_maxes, *ms))
    for t in range(tiles):
        acc0, acc1 = acc_ref[2 * t], acc_ref[2 * t + 1]
        o_t = _cat([acc0[:HEAD_DIM] / acc0[HEAD_DIM:HEAD_DIM + 1],
                    acc1[:HEAD_DIM] / acc1[HEAD_DIM:HEAD_DIM + 1]], 0)
        o_ref[0, 0, t * L:(t + 1) * L, :] = o_t.T.astype(o_ref.dtype)


def _moba_attn(sfeat, slopes, q, k, v, unroll, tiles):
    batch, _, seq, _ = q.shape
    nb = seq // MOBA_BLOCK
    assert nb % tiles == 0
    tile = pl.BlockSpec((1, 1, tiles * MOBA_BLOCK, LANES), lambda b, p, n: (b, p, n, 0))
    full = pl.BlockSpec((1, 1, seq, LANES), lambda b, p, n: (b, p, 0, 0))
    smem = pl.BlockSpec(memory_space=pltpu.SMEM)
    return pl.pallas_call(
        functools.partial(_moba_attn_kernel, nb=nb, unroll=unroll, tiles=tiles),
        grid=(batch, N_PAIRS, nb // tiles),
        in_specs=[smem, smem, tile, full, full],
        out_specs=tile,
        out_shape=jax.ShapeDtypeStruct(q.shape, BF16),
        scratch_shapes=[pltpu.VMEM((nb, LANES), F32),
                        pltpu.VMEM((2, seq, LANES), BF16),
                        pltpu.VMEM((nb, 2, V_ROWS, MOBA_BLOCK), BF16),
                        pltpu.VMEM((2, 2 * tiles * unroll, MOBA_BLOCK, MOBA_BLOCK), F32),
                        pltpu.VMEM((2 * tiles, V_ROWS, MOBA_BLOCK), F32)],
        compiler_params=_params(3),
        name="moba_attn",
    )(sfeat, slopes, q, k, v)


def _out_proj_kernel(o_ref, sg_ref, x_ref, w_ref, g_ref, y_ref, *, final_norm):
    o = _cat([o_ref[0, p] for p in range(N_PAIRS)], 1)
    gated = (o.astype(F32) * sg_ref[...].astype(F32)).astype(BF16)
    y = x_ref[...] + _dot(gated, w_ref[...])
    if final_norm:
        y = _rms(y, g_ref[...])
    y_ref[...] = y


def _out_proj(o_pairs, sg, x2d, w_bf16, norm_g, seq, tm, final_norm):
    m = x2d.shape[0]
    nsb = seq // tm
    row_spec = pl.BlockSpec((tm, D_MODEL), lambda i: (i, 0))
    return pl.pallas_call(
        functools.partial(_out_proj_kernel, final_norm=final_norm),
        grid=(m // tm,),
        in_specs=[
            pl.BlockSpec((1, N_PAIRS, tm, LANES), lambda i: (i // nsb, 0, i % nsb, 0)),
            row_spec, row_spec,
            _const_spec((WIDTH, D_MODEL)),
            _const_spec((1, D_MODEL)),
        ],
        out_specs=row_spec,
        out_shape=jax.ShapeDtypeStruct((m, D_MODEL), F32),
        compiler_params=_params(1),
        name="out_proj_final" if final_norm else "out_proj",
    )(o_pairs, sg, x2d, w_bf16, norm_g)


def _rwkv_proj_kernel(x_ref, halo_ref, g_ref, mix_ref, win_ref, w0_ref, w1h_ref, w1l_ref,
                      w2h_ref, w2l_ref, a0_ref, a1h_ref, a1l_ref, a2h_ref, a2l_ref,
                      kk_ref, ka_ref, rk_ref,
                      r_out, k_out, v_out, lw_out, kk_out, bb_out, bonus_out, sg_out,
                      *, tiles_per_seq):
    i = pl.program_id(0)
    tm = x_ref.shape[0]
    g = g_ref[...]
    h = _rms(x_ref[...], g)
    prev_row = _rms(halo_ref[...], g)[7:8, :]
    prev_row = jnp.where(i % tiles_per_seq == 0, 0.0, prev_row)
    rolled = pltpu.roll(h, 1, 0)
    h_prev = jnp.where(_iota((tm, D_MODEL), 0) == 0, prev_row, rolled)
    xx = h_prev - h

    def stream(n):
        return h + xx * mix_ref[n:n + 1, :]

    r = _dot(stream(0).astype(BF16), win_ref[0])
    k = _dot(stream(1).astype(BF16), win_ref[1])
    v = _dot(stream(2).astype(BF16), win_ref[2])
    gt = _dot(stream(3).astype(BF16), win_ref[3])
    sg_out[...] = (gt * jax.nn.sigmoid(gt)).astype(BF16)

    def lora(xs, ah_ref, al_ref, bh_ref, bl_ref, act):
        xh, xl = _split2(xs)
        mid = _dot(xh, ah_ref[...]) + _dot(xh, al_ref[...]) + _dot(xl, ah_ref[...])
        mid = act(mid)
        mh, ml = _split2(mid)
        return _dot(mh, bh_ref[...]) + _dot(mh, bl_ref[...]) + _dot(ml, bh_ref[...])

    z = -(w0_ref[...] + lora(stream(4), w1h_ref, w1l_ref, w2h_ref, w2l_ref, jnp.tanh))
    softplus = jnp.maximum(z, 0.0) + jnp.log(1.0 + jnp.exp(-jnp.abs(z)))
    lw = -jnp.exp(-softplus - 0.5)
    a = jax.nn.sigmoid(a0_ref[...] + lora(stream(5), a1h_ref, a1l_ref, a2h_ref, a2l_ref,
                                          lambda t: t))
    kr = k * kk_ref[...]
    k_mod = k * (1.0 + (a - 1.0) * ka_ref[...])
    rk = r * k_mod * rk_ref[...]
    ones_bd = _head_blockdiag(LANES).astype(BF16)
    ones_bd2 = _cat([ones_bd, ones_bd], 0)
    for p in range(N_PAIRS):
        cols = slice(p * LANES, (p + 1) * LANES)
        kr_p = kr[:, cols]
        ss = _mm2(kr_p * kr_p, ones_bd2)
        kk_p = kr_p / jnp.maximum(jnp.sqrt(ss), 1e-12)
        r_out[0, p] = r[:, cols]
        k_out[0, p] = k_mod[:, cols]
        v_out[0, p] = v[:, cols]
        lw_out[0, p] = lw[:, cols]
        kk_out[0, p] = kk_p
        bb_out[0, p] = kk_p * a[:, cols]
        bonus_out[0, p] = _mm2(rk[:, cols], ones_bd2) * v[:, cols]


def _rwkv_proj(x2d, weights, batch, seq, tm):
    m = x2d.shape[0]
    nsb = seq // tm
    halo_blocks = tm // 8
    pair_spec = pl.BlockSpec((1, N_PAIRS, tm, LANES), lambda i: (i // nsb, 0, i % nsb, 0))
    pair_shape = jax.ShapeDtypeStruct((batch, N_PAIRS, seq, LANES), F32)
    vec = _const_spec((1, WIDTH))
    lora_in = _const_spec((D_MODEL, LORA))
    lora_out = _const_spec((LORA, WIDTH))
    return pl.pallas_call(
        functools.partial(_rwkv_proj_kernel, tiles_per_seq=nsb),
        grid=(m // tm,),
        in_specs=[
            pl.BlockSpec((tm, D_MODEL), lambda i: (i, 0)),
            pl.BlockSpec((8, D_MODEL), lambda i: (jnp.maximum(i * halo_blocks - 1, 0), 0)),
            _const_spec((1, D_MODEL)),
            _const_spec((6, D_MODEL)),
            _const_spec((4, D_MODEL, WIDTH)),
            vec, lora_in, lora_in, lora_out, lora_out,
            vec, lora_in, lora_in, lora_out, lora_out,
            vec, vec, vec,
        ],
        out_specs=[pair_spec] * 7 + [pl.BlockSpec((tm, WIDTH), lambda i: (i, 0))],
        out_shape=[pair_shape] * 7 + [jax.ShapeDtypeStruct((m, WIDTH), BF16)],
        compiler_params=_params(1),
        name="rwkv_proj",
    )(x2d, x2d, *weights)


def _rwkv_scan_kernel(r_ref, k_ref, v_ref, lw_ref, kk_ref, bb_ref, bonus_ref, lng_ref, lnb_ref,
                      o_ref, z_ref, *, chunks_per_step):
    C = CHUNK
    pairs = range(N_PAIRS)

    @pl.when(pl.program_id(1) == 0)
    def _():
        z_ref[...] = jnp.zeros_like(z_ref)

    row = _iota((C, LANES), 0)
    lane = _iota((C, LANES), 1)
    col = lane % C
    head_lo = lane < HEAD_DIM
    keep_lo = head_lo.astype(BF16)
    keep_hi = jnp.logical_not(head_lo).astype(BF16)
    eye = (row == col).astype(F32)
    strict = col < row
    lower2 = (_iota((C, 2 * LANES), 1) % C) <= _iota((C, 2 * LANES), 0)
    same_block = {b: (row // b) == (col // b) for b in (8, 16, 32)}
    tri3 = ((_iota((C, 3 * C), 1) % C) <= _iota((C, 3 * C), 0)).astype(BF16)
    head_bd = _head_blockdiag(LANES)
    avg_bd = head_bd.astype(BF16) * (1.0 / HEAD_DIM)
    avg_bd2 = _cat([avg_bd, avg_bd], 0)

    def blockdiag(xb):
        return _cat([xb * keep_lo, xb * keep_hi], 0)

    def prod(a_list, b_list):
        parts = [(_split2(a), b.astype(BF16)) for a, b in zip(a_list, b_list)]
        return [_mm_left(ah, al, blockdiag(bb), "nn") for (ah, al), bb in parts]

    def prod_bf16(a_list, b_list):
        parts = [(a.astype(BF16), b.astype(BF16)) for a, b in zip(a_list, b_list)]
        return [_dot(ab, blockdiag(bb)) for ab, bb in parts]

    def state_free(c, out):
        rows = slice(c * C, (c + 1) * C)
        v = [v_ref[0, p, rows, :] for p in pairs]
        lw = [lw_ref[0, p, rows, :] for p in pairs]
        lw3 = [_cat(_split3(x), 0) for x in lw]
        cs = [_dot(tri3, x) for x in lw3]
        yield
        decay_in = [jnp.exp(x) for x in cs]
        inv = [jnp.exp(-x) for x in cs]
        r_t = [r_ref[0, p, rows, :] * decay_in[p] for p in pairs]
        k_t = [k_ref[0, p, rows, :] * inv[p] for p in pairs]
        b_t = [bb_ref[0, p, rows, :] * inv[p] for p in pairs]
        a_t = [-(kk_ref[0, p, rows, :] * jnp.exp(cs[p] - lw[p])) for p in pairs]
        ar = [_split2(_cat([a_t[p], r_t[p]], 0)) for p in pairs]
        bk_rows = [_cat([blockdiag(b_t[p].astype(BF16)), blockdiag(k_t[p].astype(BF16))], 0)
                   for p in pairs]
        mbk = [_mm_left(*ar[p], bk_rows[p], "nt") for p in pairs]
        yield
        low = [jnp.where(strict, x[:C, :LANES], 0.0) for x in mbk]
        ak = [jnp.where(strict, x[:C, LANES:], 0.0) for x in mbk]
        rbk = [jnp.where(lower2, x[C:], 0.0) for x in mbk]

        ld = [jnp.where(same_block[8], x, 0.0) for x in low]
        l2 = prod_bf16(ld, ld)
        yield
        l4 = prod_bf16(l2, l2)
        l3 = prod_bf16(ld, l2)
        yield
        p1 = [eye + ld[p] + l2[p] + l3[p] for p in pairs]
        p1l4 = prod_bf16(p1, l4)
        yield
        t = [p1[p] + p1l4[p] for p in pairs]
        b = 8
        while b < C:
            couple = jnp.logical_not(same_block[b])
            if 2 * b < C:
                couple = jnp.logical_and(same_block[2 * b], couple)
            x = [jnp.where(couple, y, 0.0) for y in low]
            tx = prod_bf16(t, x)
            yield
            txt = prod_bf16(tx, t)
            yield
            t = [t[p] + txt[p] for p in pairs]
            b *= 2
        akv = prod(ak, v)
        yield
        scale = [x[C - 1:C, :] for x in decay_in]
        out.update(
            v=v, ar=ar, rsp=[_split2(x) for x in rbk], t=t, akv=akv,
            bk=[_split2(_cat([b_t[p] * scale[p], k_t[p] * scale[p]], 0)) for p in pairs],
            decay_cols=[x[C - 8:C, :].T[:, 7:8] for x in decay_in])

    def state_step(c, pre, z):
        rows = slice(c * C, (c + 1) * C)
        st = [_mm_left(*pre["ar"][p], z[p].astype(BF16), "nn") for p in pairs]
        yield
        rhs = [st[p][:C] + pre["akv"][p] for p in pairs]
        u = prod(pre["t"], rhs)
        yield
        ub = [x.astype(BF16) for x in u]
        vb = [x.astype(BF16) for x in pre["v"]]
        uv = [_cat([blockdiag(ub[p]), blockdiag(vb[p])], 0) for p in pairs]
        y = [st[p][C:] + _mm_left(*pre["rsp"][p], uv[p], "nn") for p in pairs]
        upd = [_mm_left(*pre["bk"][p], _cat([ub[p], vb[p]], 0), "tn") for p in pairs]
        yield
        for p in pairs:
            z[p] = pre["decay_cols"][p] * z[p] + jnp.where(head_bd, upd[p], 0.0)
        mu = [_mm2(x, avg_bd2) for x in y]
        yield
        d = [y[p] - mu[p] for p in pairs]
        var = [_mm2(x * x, avg_bd2) for x in d]
        yield
        for p in pairs:
            yn = d[p] * lax.rsqrt(var[p] + LNX_EPS) * lng_ref[p] + lnb_ref[p]
            o_ref[0, p, rows, :] = (yn + bonus_ref[0, p, rows, :]).astype(o_ref.dtype)

    z = [z_ref[p] for p in pairs]
    pre = {}
    for _ in state_free(0, pre):
        pass
    for c in range(chunks_per_step):
        nxt = {}
        ahead = state_free(c + 1, nxt) if c + 1 < chunks_per_step else iter(())
        for _ in state_step(c, pre, z):
            for _ in range(SCAN_FREE_STAGES_PER_STEP_STAGE):
                next(ahead, None)
        for _ in ahead:
            pass
        pre = nxt
    for p in pairs:
        z_ref[p] = z[p]


def _rwkv_scan(r, k, v, lw, kk, bb, bonus, lnx_g, lnx_b, chunks_per_step):
    batch, _, seq, _ = r.shape
    tc = CHUNK * chunks_per_step
    blk = pl.BlockSpec((1, N_PAIRS, tc, LANES), lambda b, c: (b, 0, c, 0))
    vec = _const_spec((N_PAIRS, 1, LANES))
    return pl.pallas_call(
        functools.partial(_rwkv_scan_kernel, chunks_per_step=chunks_per_step),
        grid=(batch, seq // tc),
        in_specs=[blk] * 7 + [vec, vec],
        out_specs=blk,
        out_shape=jax.ShapeDtypeStruct(r.shape, BF16),
        scratch_shapes=[pltpu.VMEM((N_PAIRS, LANES, LANES), F32)],
        compiler_params=_params(2),
        name="rwkv_scan",
    )(r, k, v, lw, kk, bb, bonus, lnx_g, lnx_b)


def _hi_lo(w):
    hi = w.astype(BF16)
    lo = (w - hi.astype(F32)).astype(BF16)
    return hi, lo


def kernel(x, moba_norm_g, moba_w_in, moba_w_out, rwkv_norm_g, rwkv_mix, rwkv_w_in, rwkv_w0, rwkv_w1, rwkv_w2, rwkv_a0, rwkv_a1, rwkv_a2, rwkv_k_k, rwkv_k_a, rwkv_r_k, rwkv_lnx_g, rwkv_lnx_b, rwkv_w_out, final_norm_g):
    batch, seq, d = x.shape
    assert d == D_MODEL and seq % MOBA_BLOCK == 0 and seq // MOBA_BLOCK <= 32
    x2d = x.reshape(batch * seq, d)
    row = lambda t: t.reshape(1, -1).astype(F32)

    q, k, v, sg = _moba_proj(x2d, row(moba_norm_g), moba_w_in.astype(BF16), batch, seq, tm=512)
    slopes = jnp.asarray([2.0 ** (-8.0 * (i + 1) / N_HEADS) for i in range(N_HEADS)], F32)
    slope_hi = (slopes * LOG2E).astype(BF16).astype(F32)
    sfeat = jnp.stack([slope_hi, slopes * LOG2E - slope_hi], axis=1).reshape(-1)
    o = _moba_attn(sfeat, slopes, q, k, v, unroll=2, tiles=2)
    x1 = _out_proj(o, sg, x2d, moba_w_out.astype(BF16), row(final_norm_g), seq, tm=512,
                   final_norm=False)

    w1h, w1l = _hi_lo(rwkv_w1)
    w2h, w2l = _hi_lo(rwkv_w2)
    a1h, a1l = _hi_lo(rwkv_a1)
    a2h, a2l = _hi_lo(rwkv_a2)
    weights = (row(rwkv_norm_g), rwkv_mix.astype(F32), rwkv_w_in.astype(BF16),
               row(rwkv_w0), w1h, w1l, w2h, w2l, row(rwkv_a0), a1h, a1l, a2h, a2l,
               row(rwkv_k_k), row(rwkv_k_a), row(rwkv_r_k))
    r, k2, v2, lw, kk, bb, bonus, sg2 = _rwkv_proj(x1, weights, batch, seq, tm=256)
    pair_vec = lambda t: t.reshape(N_PAIRS, 1, LANES).astype(F32)
    o2 = _rwkv_scan(r, k2, v2, lw, kk, bb, bonus, pair_vec(rwkv_lnx_g), pair_vec(rwkv_lnx_b),
                    chunks_per_step=4)
    out = _out_proj(o2, sg2, x1, rwkv_w_out.astype(BF16), row(final_norm_g), seq, tm=512,
                    final_norm=True)
    return out.reshape(batch, seq, d)
```

```python
import functools
import math

import jax
import jax.numpy as jnp
from jax import lax
from jax.experimental import pallas as pl
from jax.experimental.pallas import tpu as pltpu

F32 = jnp.float32
BF16 = jnp.bfloat16

D_MODEL = 1024
N_HEADS = 16
HEAD_DIM = 64
WIDTH = N_HEADS * HEAD_DIM
LANES = 128
N_PAIRS = WIDTH // LANES
MOBA_BLOCK = 256
MOBA_TOPK = 3
BF16_SUBLANES = 16
V_ROWS = HEAD_DIM + BF16_SUBLANES
LORA = 64
NORM_EPS = 1e-6
LNX_EPS = 64e-5
CHUNK = 64
SCAN_FREE_STAGES_PER_STEP_STAGE = 3
VMEM_LIMIT = 56 * 1024 * 1024
NEG_INF = float("-inf")
POS_INF = float("inf")
LOG2E = math.log2(math.e)


def _dot(a, b):
    return jnp.dot(a, b, preferred_element_type=F32)


def _dot_nt(a, b):
    return lax.dot_general(a, b, (((1,), (1,)), ((), ())), preferred_element_type=F32)


def _dot_tn(a, b):
    return lax.dot_general(a, b, (((0,), (0,)), ((), ())), preferred_element_type=F32)


def _split2(x):
    hi = x.astype(BF16)
    lo = (x - hi.astype(F32)).astype(BF16)
    return hi, lo


def _split3(x):
    hi = x.astype(BF16)
    r1 = x - hi.astype(F32)
    mid = r1.astype(BF16)
    lo = (r1 - mid.astype(F32)).astype(BF16)
    return hi, mid, lo


def _cat(xs, axis):
    return jnp.concatenate(xs, axis=axis)


def _mm3_parts(ah, al, bh, bl, kind):
    if kind == "nn":
        return _dot(_cat([ah, al], 1), _cat([bh, bh], 0)) + _dot(ah, bl)
    if kind == "nt":
        return _dot_nt(_cat([ah, al], 1), _cat([bh, bh], 1)) + _dot_nt(ah, bl)
    assert kind == "tn"
    return _dot_tn(_cat([ah, al], 0), _cat([bh, bh], 0)) + _dot_tn(ah, bl)


def _mm_left(ah, al, b, kind):
    if kind == "nn":
        return _dot(_cat([ah, al], 1), _cat([b, b], 0))
    if kind == "nt":
        return _dot_nt(_cat([ah, al], 1), _cat([b, b], 1))
    assert kind == "tn"
    return _dot_tn(_cat([ah, al], 0), _cat([b, b], 0))


def _mm3(a, b, kind="nn"):
    ah, al = _split2(a)
    bh, bl = _split2(b)
    return _mm3_parts(ah, al, bh, bl, kind)


def _mm2(a, b2_exact):
    ah, al = _split2(a)
    return _dot(_cat([ah, al], 1), b2_exact)


def _rms(x, g):
    ms = jnp.mean(x * x, axis=-1, keepdims=True)
    return x * lax.rsqrt(ms + NORM_EPS) * g


def _iota(shape, dim):
    return lax.broadcasted_iota(jnp.int32, shape, dim)


def _head_blockdiag(n):
    return (_iota((n, n), 0) // HEAD_DIM) == (_iota((n, n), 1) // HEAD_DIM)


def _const_spec(shape):
    nd = len(shape)
    return pl.BlockSpec(shape, lambda *_: (0,) * nd)


def _params(n_axes):
    return pltpu.CompilerParams(dimension_semantics=("arbitrary",) * n_axes,
                                vmem_limit_bytes=VMEM_LIMIT)


def _moba_proj_kernel(x_ref, g_ref, w_ref, q_ref, k_ref, v_ref, sg_ref):
    h = _rms(x_ref[...], g_ref[...]).astype(BF16)
    q = _dot(h, w_ref[:, 0 * WIDTH:1 * WIDTH])
    k = _dot(h, w_ref[:, 1 * WIDTH:2 * WIDTH])
    v = _dot(h, w_ref[:, 2 * WIDTH:3 * WIDTH])
    gate = _dot(h, w_ref[:, 3 * WIDTH:4 * WIDTH])
    for p in range(N_PAIRS):
        cols = slice(p * LANES, (p + 1) * LANES)
        q_ref[0, p] = q[:, cols]
        k_ref[0, p] = k[:, cols].astype(BF16)
        v_ref[0, p] = v[:, cols].astype(BF16)
    sg_ref[...] = (gate * jax.nn.sigmoid(gate)).astype(BF16)


def _moba_proj(x2d, norm_g, w_bf16, batch, seq, tm):
    m = x2d.shape[0]
    nsb = seq // tm
    pair_spec = pl.BlockSpec((1, N_PAIRS, tm, LANES), lambda i: (i // nsb, 0, i % nsb, 0))
    pair_shape = (batch, N_PAIRS, seq, LANES)
    return pl.pallas_call(
        _moba_proj_kernel,
        grid=(m // tm,),
        in_specs=[
            pl.BlockSpec((tm, D_MODEL), lambda i: (i, 0)),
            _const_spec((1, D_MODEL)),
            _const_spec((D_MODEL, 4 * WIDTH)),
        ],
        out_specs=[pair_spec, pair_spec, pair_spec,
                   pl.BlockSpec((tm, WIDTH), lambda i: (i, 0))],
        out_shape=[
            jax.ShapeDtypeStruct(pair_shape, F32),
            jax.ShapeDtypeStruct(pair_shape, BF16),
            jax.ShapeDtypeStruct(pair_shape, BF16),
            jax.ShapeDtypeStruct((m, WIDTH), BF16),
        ],
        compiler_params=_params(1),
        name="moba_proj",
    )(x2d, norm_g, w_bf16)


def _moba_attn_kernel(sfeat_ref, slopes_ref, q_ref, k_ref, v_ref, o_ref,
                      kmean_ref, kaug_ref, vt_ref, s_ref, acc_ref, *, nb, unroll, tiles):
    L = MOBA_BLOCK
    p = pl.program_id(1)
    n = pl.program_id(2)
    lane = _iota((L, LANES), 1)
    head_lo = lane < HEAD_DIM
    head_masks = (head_lo, jnp.logical_not(head_lo))
    feat_base = (HEAD_DIM, 0)

    @pl.when(n == 0)
    def _():
        offs = _iota((L, LANES), 0).astype(F32)
        ones_row = (_iota((V_ROWS - HEAD_DIM, L), 0) == 0).astype(BF16)
        for j in range(nb):
            rows = slice(j * L, (j + 1) * L)
            kf = k_ref[0, 0, rows, :].astype(F32)
            kmean_ref[j:j + 1, :] = jnp.mean(kf, axis=0, keepdims=True)
            for hh in range(2):
                fb = feat_base[hh]
                feat = jnp.where(jnp.logical_or(lane == fb, lane == fb + 1), offs, 0.0)
                kaug_ref[hh, rows, :] = jnp.where(head_masks[hh], kf, feat).astype(BF16)
            v_t = v_ref[0, 0, rows, :].astype(F32).T.astype(BF16)
            for hh in range(2):
                vt_ref[j, hh, :HEAD_DIM, :] = v_t[hh * HEAD_DIM:(hh + 1) * HEAD_DIM, :]
                vt_ref[j, hh, HEAD_DIM:, :] = ones_row

    n_first = n * tiles
    units = [(t, hh) for t in range(tiles) for hh in range(2)]
    blk = _iota((nb, L), 0)
    kmean = kmean_ref[...]
    key_i = _iota((L, L), 0)
    qry_i = _iota((L, L), 1)

    qa, bits = [], []
    for t, hh in units:
        h = 2 * p + hh
        qf = q_ref[0, 0, t * L:(t + 1) * L, :]
        gate = _mm3(kmean, jnp.where(head_masks[hh], qf, 0.0), "nt")
        gate = jnp.where(blk < n_first + t, gate, NEG_INF)
        chosen_bits = jnp.zeros((1, L), jnp.int32)
        for _ in range(MOBA_TOPK):
            mx = jnp.max(gate, axis=0, keepdims=True)
            first = jnp.min(jnp.where(gate == mx, blk, nb), axis=0, keepdims=True)
            valid = mx > NEG_INF
            chosen_bits = chosen_bits | jnp.where(valid, jnp.left_shift(1, first), 0)
            gate = jnp.where(jnp.logical_and(blk == first, valid), NEG_INF, gate)
        bits.append(chosen_bits)
        fb = feat_base[hh]
        feat = jnp.where(lane == fb, sfeat_ref[2 * h],
                         jnp.where(lane == fb + 1, sfeat_ref[2 * h + 1], 0.0))
        qa.append(jnp.where(head_masks[hh], qf * (HEAD_DIM ** -0.5 * LOG2E), feat).astype(BF16))
    qa_all = [_cat([qa[2 * t + hh] for t in range(tiles)], 0) for hh in range(2)]

    def scores(j):
        off = pl.multiple_of(j * L, L)
        per_head = [_dot_nt(kaug_ref[hh, pl.ds(off, L), :], qa_all[hh]) for hh in range(2)]
        return [per_head[hh][:, t * L:(t + 1) * L] for t, hh in units]

    nu = len(units)

    def issue_scores(g, slot):
        maxes = []
        for u in range(unroll):
            for w, s in enumerate(scores(jnp.minimum(g * unroll + u, nb - 1))):
                s_ref[slot, nu * u + w] = s
                maxes.append(jnp.max(s, axis=0, keepdims=True))
        return maxes

    def consume(g, slot, maxes, ms):
        ms = list(ms)
        for u in range(unroll):
            j = g * unroll + u
            jc = jnp.minimum(j, nb - 1)
            for w, (t, hh) in enumerate(units):
                tile = n_first + t
                c = ((j - tile) * L).astype(F32) * LOG2E * slopes_ref[2 * p + hh]
                chosen = jnp.logical_and((jnp.right_shift(bits[w], jc) & 1) == 1, j < tile)
                m_new = jnp.maximum(ms[w], jnp.where(chosen, maxes[nu * u + w] + c, NEG_INF))
                alpha = jnp.exp2(ms[w] - m_new)
                shift = jnp.where(chosen, m_new - c, POS_INF)
                pr = jnp.exp2(s_ref[slot, nu * u + w] - shift).astype(BF16)
                acc_ref[w] = alpha * acc_ref[w] + _dot(vt_ref[jc, hh], pr)
                ms[w] = m_new
        return ms

    own = [_dot_nt(kaug_ref[hh, pl.ds(pl.multiple_of((n_first + t) * L, L), L), :], qa[2 * t + hh])
           for t, hh in units]
    first_maxes = issue_scores(0, 0)
    ms = []
    for w, (t, hh) in enumerate(units):
        tile = n_first + t
        s = jnp.where(key_i <= qry_i, own[w], NEG_INF)
        m = jnp.max(s, axis=0, keepdims=True)
        pr = jnp.exp2(s - m).astype(BF16)
        acc_ref[w] = _dot(vt_ref[tile, hh], pr)
        ms.append(m)

    nm = nu * unroll

    def body(i, carry):
        maxes, ms = carry[:nm], carry[nm:]
        maxes_b = issue_scores(2 * i + 1, 1)
        ms = consume(2 * i, 0, maxes, ms)
        maxes_a = issue_scores(2 * i + 2, 0)
        ms = consume(2 * i + 1, 1, maxes_b, ms)
        return (*maxes_a, *ms)

    per_trip = 2 * unroll
    n_past = n_first + tiles - 1
    lax.fori_loop(0, (n_past + per_trip - 1) // per_trip, body, (*first_maxes, *ms))
    for t in range(tiles):
        acc0, acc1 = acc_ref[2 * t], acc_ref[2 * t + 1]
        o_t = _cat([acc0[:HEAD_DIM] / acc0[HEAD_DIM:HEAD_DIM + 1],
                    acc1[:HEAD_DIM] / acc1[HEAD_DIM:HEAD_DIM + 1]], 0)
        o_ref[0, 0, t * L:(t + 1) * L, :] = o_t.T.astype(o_ref.dtype)


def _moba_attn(sfeat, slopes, q, k, v, unroll, tiles):
    batch, _, seq, _ = q.shape
    nb = seq // MOBA_BLOCK
    assert nb % tiles == 0
    tile = pl.BlockSpec((1, 1, tiles * MOBA_BLOCK, LANES), lambda b, p, n: (b, p, n, 0))
    full = pl.BlockSpec((1, 1, seq, LANES), lambda b, p, n: (b, p, 0, 0))
    smem = pl.BlockSpec(memory_space=pltpu.SMEM)
    return pl.pallas_call(
        functools.partial(_moba_attn_kernel, nb=nb, unroll=unroll, tiles=tiles),
        grid=(batch, N_PAIRS, nb // tiles),
        in_specs=[smem, smem, tile, full, full],
        out_specs=tile,
        out_shape=jax.ShapeDtypeStruct(q.shape, BF16),
        scratch_shapes=[pltpu.VMEM((nb, LANES), F32),
                        pltpu.VMEM((2, seq, LANES), BF16),
                        pltpu.VMEM((nb, 2, V_ROWS, MOBA_BLOCK), BF16),
                        pltpu.VMEM((2, 2 * tiles * unroll, MOBA_BLOCK, MOBA_BLOCK), F32),
                        pltpu.VMEM((2 * tiles, V_ROWS, MOBA_BLOCK), F32)],
        compiler_params=_params(3),
        name="moba_attn",
    )(sfeat, slopes, q, k, v)


def _out_proj_kernel(o_ref, sg_ref, x_ref, w_ref, g_ref, y_ref, *, final_norm):
    o = _cat([o_ref[0, p] for p in range(N_PAIRS)], 1)
    gated = (o.astype(F32) * sg_ref[...].astype(F32)).astype(BF16)
    y = x_ref[...] + _dot(gated, w_ref[...])
    if final_norm:
        y = _rms(y, g_ref[...])
    y_ref[...] = y


def _out_proj(o_pairs, sg, x2d, w_bf16, norm_g, seq, tm, final_norm):
    m = x2d.shape[0]
    nsb = seq // tm
    row_spec = pl.BlockSpec((tm, D_MODEL), lambda i: (i, 0))
    return pl.pallas_call(
        functools.partial(_out_proj_kernel, final_norm=final_norm),
        grid=(m // tm,),
        in_specs=[
            pl.BlockSpec((1, N_PAIRS, tm, LANES), lambda i: (i // nsb, 0, i % nsb, 0)),
            row_spec, row_spec,
            _const_spec((WIDTH, D_MODEL)),
            _const_spec((1, D_MODEL)),
        ],
        out_specs=row_spec,
        out_shape=jax.ShapeDtypeStruct((m, D_MODEL), F32),
        compiler_params=_params(1),
        name="out_proj_final" if final_norm else "out_proj",
    )(o_pairs, sg, x2d, w_bf16, norm_g)


def _rwkv_proj_kernel(x_ref, halo_ref, g_ref, mix_ref, win_ref, w0_ref, w1t_ref, w2_ref,
                      a0_ref, a1t_ref, a2_ref, kk_ref, ka_ref, rk_ref,
                      r_out, k_out, v_out, lw_out, kk_out, bb_out, bonus_out, sg_out,
                      *, tiles_per_seq):
    i = pl.program_id(0)
    tm = x_ref.shape[0]
    g = g_ref[...]
    h = _rms(x_ref[...], g)
    prev_row = _rms(halo_ref[...], g)[7:8, :]
    prev_row = jnp.where(i % tiles_per_seq == 0, 0.0, prev_row)
    rolled = pltpu.roll(h, 1, 0)
    h_prev = jnp.where(_iota((tm, D_MODEL), 0) == 0, prev_row, rolled)
    xx = h_prev - h

    def stream(n):
        return h + xx * mix_ref[n:n + 1, :]

    r = _dot(stream(0).astype(BF16), win_ref[0])
    k = _dot(stream(1).astype(BF16), win_ref[1])
    v = _dot(stream(2).astype(BF16), win_ref[2])
    gt = _dot(stream(3).astype(BF16), win_ref[3])
    sg_out[...] = (gt * jax.nn.sigmoid(gt)).astype(BF16)

    def lora(xs, down_t_ref, up_ref, act):
        mid_t = act(_dot_nt(down_t_ref[...], xs.astype(BF16)))
        return _dot_tn(mid_t.astype(BF16), up_ref[...])

    z = -(w0_ref[...] + lora(stream(4), w1t_ref, w2_ref, jnp.tanh))
    softplus = jnp.maximum(z, 0.0) + jnp.log(1.0 + jnp.exp(-jnp.abs(z)))
    lw = -jnp.exp(-softplus - 0.5)
    a = jax.nn.sigmoid(a0_ref[...] + lora(stream(5), a1t_ref, a2_ref, lambda t: t))
    kr = k * kk_ref[...]
    k_mod = k * (1.0 + (a - 1.0) * ka_ref[...])
    rk = r * k_mod * rk_ref[...]
    ones_bd = _head_blockdiag(LANES).astype(BF16)
    ones_bd2 = _cat([ones_bd, ones_bd], 0)
    for p in range(N_PAIRS):
        cols = slice(p * LANES, (p + 1) * LANES)
        kr_p = kr[:, cols]
        ss = _mm2(kr_p * kr_p, ones_bd2)
        kk_p = kr_p * jnp.minimum(lax.rsqrt(ss), 1e12)
        r_out[0, p] = r[:, cols]
        k_out[0, p] = k_mod[:, cols]
        v_out[0, p] = v[:, cols]
        lw_out[0, p] = lw[:, cols]
        kk_out[0, p] = kk_p
        bb_out[0, p] = kk_p * a[:, cols]
        bonus_out[0, p] = _mm2(rk[:, cols], ones_bd2) * v[:, cols]


def _rwkv_proj(x2d, weights, batch, seq, tm):
    m = x2d.shape[0]
    nsb = seq // tm
    halo_blocks = tm // 8
    pair_spec = pl.BlockSpec((1, N_PAIRS, tm, LANES), lambda i: (i // nsb, 0, i % nsb, 0))
    pair_shape = jax.ShapeDtypeStruct((batch, N_PAIRS, seq, LANES), F32)
    vec = _const_spec((1, WIDTH))
    lora_w = _const_spec((LORA, WIDTH))
    return pl.pallas_call(
        functools.partial(_rwkv_proj_kernel, tiles_per_seq=nsb),
        grid=(m // tm,),
        in_specs=[
            pl.BlockSpec((tm, D_MODEL), lambda i: (i, 0)),
            pl.BlockSpec((8, D_MODEL), lambda i: (jnp.maximum(i * halo_blocks - 1, 0), 0)),
            _const_spec((1, D_MODEL)),
            _const_spec((6, D_MODEL)),
            _const_spec((4, D_MODEL, WIDTH)),
            vec, lora_w, lora_w,
            vec, lora_w, lora_w,
            vec, vec, vec,
        ],
        out_specs=[pair_spec] * 7 + [pl.BlockSpec((tm, WIDTH), lambda i: (i, 0))],
        out_shape=[pair_shape] * 7 + [jax.ShapeDtypeStruct((m, WIDTH), BF16)],
        compiler_params=_params(1),
        name="rwkv_proj",
    )(x2d, x2d, *weights)


def _rwkv_scan_kernel(r_ref, k_ref, v_ref, lw_ref, kk_ref, bb_ref, bonus_ref, lng_ref, lnb_ref,
                      o_ref, z_ref, *, chunks_per_step):
    C = CHUNK
    pairs = range(N_PAIRS)

    @pl.when(pl.program_id(1) == 0)
    def _():
        z_ref[...] = jnp.zeros_like(z_ref)

    row = _iota((C, LANES), 0)
    lane = _iota((C, LANES), 1)
    col = lane % C
    head_lo = lane < HEAD_DIM
    keep_lo = head_lo.astype(BF16)
    keep_hi = jnp.logical_not(head_lo).astype(BF16)
    eye = (row == col).astype(F32)
    strict = col < row
    lower2 = (_iota((C, 2 * LANES), 1) % C) <= _iota((C, 2 * LANES), 0)
    same_block = {b: (row // b) == (col // b) for b in (8, 16, 32)}
    tri3 = ((_iota((C, 3 * C), 1) % C) <= _iota((C, 3 * C), 0)).astype(BF16)
    head_bd = _head_blockdiag(LANES)
    avg_bd = head_bd.astype(BF16) * (1.0 / HEAD_DIM)
    avg_bd2 = _cat([avg_bd, avg_bd], 0)

    def blockdiag(xb):
        return _cat([xb * keep_lo, xb * keep_hi], 0)

    def prod(a_list, b_list):
        parts = [(_split2(a), b.astype(BF16)) for a, b in zip(a_list, b_list)]
        return [_mm_left(ah, al, blockdiag(bb), "nn") for (ah, al), bb in parts]

    def prod_bf16(a_list, b_list):
        parts = [(a.astype(BF16), b.astype(BF16)) for a, b in zip(a_list, b_list)]
        return [_dot(ab, blockdiag(bb)) for ab, bb in parts]

    def state_free(c, out):
        rows = slice(c * C, (c + 1) * C)
        v = [v_ref[0, p, rows, :] for p in pairs]
        lw = [lw_ref[0, p, rows, :] for p in pairs]
        lw3 = [_cat(_split3(x), 0) for x in lw]
        cs = [_dot(tri3, x) for x in lw3]
        yield
        decay_in = [jnp.exp(x) for x in cs]
        inv = [jnp.exp(-x) for x in cs]
        r_t = [r_ref[0, p, rows, :] * decay_in[p] for p in pairs]
        k_t = [k_ref[0, p, rows, :] * inv[p] for p in pairs]
        b_t = [bb_ref[0, p, rows, :] * inv[p] for p in pairs]
        a_t = [-(kk_ref[0, p, rows, :] * jnp.exp(cs[p] - lw[p])) for p in pairs]
        ar = [_split2(_cat([a_t[p], r_t[p]], 0)) for p in pairs]
        bk_rows = [_cat([blockdiag(b_t[p].astype(BF16)), blockdiag(k_t[p].astype(BF16))], 0)
                   for p in pairs]
        mbk = [_mm_left(*ar[p], bk_rows[p], "nt") for p in pairs]
        yield
        low = [jnp.where(strict, x[:C, :LANES], 0.0) for x in mbk]
        ak = [jnp.where(strict, x[:C, LANES:], 0.0) for x in mbk]
        rbk = [jnp.where(lower2, x[C:], 0.0) for x in mbk]

        ld = [jnp.where(same_block[8], x, 0.0) for x in low]
        l2 = prod_bf16(ld, ld)
        yield
        l4 = prod_bf16(l2, l2)
        l3 = prod_bf16(ld, l2)
        yield
        p1 = [eye + ld[p] + l2[p] + l3[p] for p in pairs]
        p1l4 = prod_bf16(p1, l4)
        yield
        t = [p1[p] + p1l4[p] for p in pairs]
        b = 8
        while b < C:
            couple = jnp.logical_not(same_block[b])
            if 2 * b < C:
                couple = jnp.logical_and(same_block[2 * b], couple)
            x = [jnp.where(couple, y, 0.0) for y in low]
            tx = prod_bf16(t, x)
            yield
            txt = prod_bf16(tx, t)
            yield
            t = [t[p] + txt[p] for p in pairs]
            b *= 2
        akv = prod(ak, v)
        yield
        scale = [x[C - 1:C, :] for x in decay_in]
        out.update(
            v=v, ar=ar, rsp=[_split2(x) for x in rbk], t=t, akv=akv,
            bk=[_split2(_cat([b_t[p] * scale[p], k_t[p] * scale[p]], 0)) for p in pairs],
            decay_cols=[x[C - 8:C, :].T[:, 7:8] for x in decay_in])

    def state_step(c, pre, z):
        rows = slice(c * C, (c + 1) * C)
        st = [_mm_left(*pre["ar"][p], z[p].astype(BF16), "nn") for p in pairs]
        yield
        rhs = [st[p][:C] + pre["akv"][p] for p in pairs]
        u = prod(pre["t"], rhs)
        yield
        ub = [x.astype(BF16) for x in u]
        vb = [x.astype(BF16) for x in pre["v"]]
        uv = [_cat([blockdiag(ub[p]), blockdiag(vb[p])], 0) for p in pairs]
        y = [st[p][C:] + _mm_left(*pre["rsp"][p], uv[p], "nn") for p in pairs]
        upd = [_mm_left(*pre["bk"][p], _cat([ub[p], vb[p]], 0), "tn") for p in pairs]
        yield
        for p in pairs:
            z[p] = pre["decay_cols"][p] * z[p] + jnp.where(head_bd, upd[p], 0.0)
        mu = [_mm2(x, avg_bd2) for x in y]
        yield
        d = [y[p] - mu[p] for p in pairs]
        var = [_mm2(x * x, avg_bd2) for x in d]
        yield
        for p in pairs:
            yn = d[p] * lax.rsqrt(var[p] + LNX_EPS) * lng_ref[p] + lnb_ref[p]
            o_ref[0, p, rows, :] = (yn + bonus_ref[0, p, rows, :]).astype(o_ref.dtype)

    z = [z_ref[p] for p in pairs]
    pre = {}
    for _ in state_free(0, pre):
        pass
    for c in range(chunks_per_step):
        nxt = {}
        ahead = state_free(c + 1, nxt) if c + 1 < chunks_per_step else iter(())
        for _ in state_step(c, pre, z):
            for _ in range(SCAN_FREE_STAGES_PER_STEP_STAGE):
                next(ahead, None)
        for _ in ahead:
            pass
        pre = nxt
    for p in pairs:
        z_ref[p] = z[p]


def _rwkv_scan(r, k, v, lw, kk, bb, bonus, lnx_g, lnx_b, chunks_per_step):
    batch, _, seq, _ = r.shape
    tc = CHUNK * chunks_per_step
    blk = pl.BlockSpec((1, N_PAIRS, tc, LANES), lambda b, c: (b, 0, c, 0))
    vec = _const_spec((N_PAIRS, 1, LANES))
    return pl.pallas_call(
        functools.partial(_rwkv_scan_kernel, chunks_per_step=chunks_per_step),
        grid=(batch, seq // tc),
        in_specs=[blk] * 7 + [vec, vec],
        out_specs=blk,
        out_shape=jax.ShapeDtypeStruct(r.shape, BF16),
        scratch_shapes=[pltpu.VMEM((N_PAIRS, LANES, LANES), F32)],
        compiler_params=_params(2),
        name="rwkv_scan",
    )(r, k, v, lw, kk, bb, bonus, lnx_g, lnx_b)


def kernel(x, moba_norm_g, moba_w_in, moba_w_out, rwkv_norm_g, rwkv_mix, rwkv_w_in, rwkv_w0, rwkv_w1, rwkv_w2, rwkv_a0, rwkv_a1, rwkv_a2, rwkv_k_k, rwkv_k_a, rwkv_r_k, rwkv_lnx_g, rwkv_lnx_b, rwkv_w_out, final_norm_g):
    batch, seq, d = x.shape
    assert d == D_MODEL and seq % MOBA_BLOCK == 0 and seq // MOBA_BLOCK <= 32
    x2d = x.reshape(batch * seq, d)
    row = lambda t: t.reshape(1, -1).astype(F32)

    q, k, v, sg = _moba_proj(x2d, row(moba_norm_g), moba_w_in.astype(BF16), batch, seq, tm=512)
    slopes = jnp.asarray([2.0 ** (-8.0 * (i + 1) / N_HEADS) for i in range(N_HEADS)], F32)
    slope_hi = (slopes * LOG2E).astype(BF16).astype(F32)
    sfeat = jnp.stack([slope_hi, slopes * LOG2E - slope_hi], axis=1).reshape(-1)
    o = _moba_attn(sfeat, slopes, q, k, v, unroll=2, tiles=4)
    x1 = _out_proj(o, sg, x2d, moba_w_out.astype(BF16), row(final_norm_g), seq, tm=512,
                   final_norm=False)

    weights = (row(rwkv_norm_g), rwkv_mix.astype(F32), rwkv_w_in.astype(BF16),
               row(rwkv_w0), rwkv_w1.T.astype(BF16), rwkv_w2.astype(BF16),
               row(rwkv_a0), rwkv_a1.T.astype(BF16), rwkv_a2.astype(BF16),
               row(rwkv_k_k), row(rwkv_k_a), row(rwkv_r_k))
    r, k2, v2, lw, kk, bb, bonus, sg2 = _rwkv_proj(x1, weights, batch, seq, tm=256)
    pair_vec = lambda t: t.reshape(N_PAIRS, 1, LANES).astype(F32)
    o2 = _rwkv_scan(r, k2, v2, lw, kk, bb, bonus, pair_vec(rwkv_lnx_g), pair_vec(rwkv_lnx_b),
                    chunks_per_step=4)
    out = _out_proj(o2, sg2, x1, rwkv_w_out.astype(BF16), row(final_norm_g), seq, tm=512,
                    final_norm=True)
    return out.reshape(batch, seq, d)
```

```python
import functools
import math

import jax
import jax.numpy as jnp
from jax import lax
from jax.experimental import pallas as pl
from jax.experimental.pallas import tpu as pltpu

F32 = jnp.float32
BF16 = jnp.bfloat16

D_MODEL = 1024
N_HEADS = 16
HEAD_DIM = 64
WIDTH = N_HEADS * HEAD_DIM
LANES = 128
N_PAIRS = WIDTH // LANES
MOBA_BLOCK = 256
MOBA_TOPK = 3
BF16_SUBLANES = 16
V_ROWS = HEAD_DIM + BF16_SUBLANES
LORA = 64
NORM_EPS = 1e-6
LNX_EPS = 64e-5
CHUNK = 64
SCAN_CHUNKS_AHEAD = 3
VMEM_LIMIT = 56 * 1024 * 1024
NEG_INF = float("-inf")
POS_INF = float("inf")
LOG2E = math.log2(math.e)


def _dot(a, b):
    return jnp.dot(a, b, preferred_element_type=F32)


def _dot_nt(a, b):
    return lax.dot_general(a, b, (((1,), (1,)), ((), ())), preferred_element_type=F32)


def _dot_tn(a, b):
    return lax.dot_general(a, b, (((0,), (0,)), ((), ())), preferred_element_type=F32)


def _split2(x):
    hi = x.astype(BF16)
    lo = (x - hi.astype(F32)).astype(BF16)
    return hi, lo


def _split3(x):
    hi = x.astype(BF16)
    r1 = x - hi.astype(F32)
    mid = r1.astype(BF16)
    lo = (r1 - mid.astype(F32)).astype(BF16)
    return hi, mid, lo


def _cat(xs, axis):
    return jnp.concatenate(xs, axis=axis)


def _mm3_parts(ah, al, bh, bl, kind):
    if kind == "nn":
        return _dot(_cat([ah, al], 1), _cat([bh, bh], 0)) + _dot(ah, bl)
    if kind == "nt":
        return _dot_nt(_cat([ah, al], 1), _cat([bh, bh], 1)) + _dot_nt(ah, bl)
    assert kind == "tn"
    return _dot_tn(_cat([ah, al], 0), _cat([bh, bh], 0)) + _dot_tn(ah, bl)


def _mm_left(ah, al, b, kind):
    if kind == "nn":
        return _dot(_cat([ah, al], 1), _cat([b, b], 0))
    if kind == "nt":
        return _dot_nt(_cat([ah, al], 1), _cat([b, b], 1))
    assert kind == "tn"
    return _dot_tn(_cat([ah, al], 0), _cat([b, b], 0))


def _mm3(a, b, kind="nn"):
    ah, al = _split2(a)
    bh, bl = _split2(b)
    return _mm3_parts(ah, al, bh, bl, kind)


def _mm2(a, b2_exact):
    ah, al = _split2(a)
    return _dot(_cat([ah, al], 1), b2_exact)


def _rms(x, g):
    ms = jnp.mean(x * x, axis=-1, keepdims=True)
    return x * lax.rsqrt(ms + NORM_EPS) * g


def _iota(shape, dim):
    return lax.broadcasted_iota(jnp.int32, shape, dim)


def _head_blockdiag(n):
    return (_iota((n, n), 0) // HEAD_DIM) == (_iota((n, n), 1) // HEAD_DIM)


def _const_spec(shape):
    nd = len(shape)
    return pl.BlockSpec(shape, lambda *_: (0,) * nd)


def _params(n_axes):
    return pltpu.CompilerParams(dimension_semantics=("arbitrary",) * n_axes,
                                vmem_limit_bytes=VMEM_LIMIT)


def _moba_proj_kernel(x_ref, g_ref, w_ref, q_ref, k_ref, v_ref, sg_ref):
    h = _rms(x_ref[...], g_ref[...]).astype(BF16)
    q = _dot(h, w_ref[:, 0 * WIDTH:1 * WIDTH])
    k = _dot(h, w_ref[:, 1 * WIDTH:2 * WIDTH])
    v = _dot(h, w_ref[:, 2 * WIDTH:3 * WIDTH])
    gate = _dot(h, w_ref[:, 3 * WIDTH:4 * WIDTH])
    for p in range(N_PAIRS):
        cols = slice(p * LANES, (p + 1) * LANES)
        q_ref[0, p] = q[:, cols]
        k_ref[0, p] = k[:, cols].astype(BF16)
        v_ref[0, p] = v[:, cols].astype(BF16)
    sg_ref[...] = (gate * jax.nn.sigmoid(gate)).astype(BF16)


def _moba_proj(x2d, norm_g, w_bf16, batch, seq, tm):
    m = x2d.shape[0]
    nsb = seq // tm
    pair_spec = pl.BlockSpec((1, N_PAIRS, tm, LANES), lambda i: (i // nsb, 0, i % nsb, 0))
    pair_shape = (batch, N_PAIRS, seq, LANES)
    return pl.pallas_call(
        _moba_proj_kernel,
        grid=(m // tm,),
        in_specs=[
            pl.BlockSpec((tm, D_MODEL), lambda i: (i, 0)),
            _const_spec((1, D_MODEL)),
            _const_spec((D_MODEL, 4 * WIDTH)),
        ],
        out_specs=[pair_spec, pair_spec, pair_spec,
                   pl.BlockSpec((tm, WIDTH), lambda i: (i, 0))],
        out_shape=[
            jax.ShapeDtypeStruct(pair_shape, F32),
            jax.ShapeDtypeStruct(pair_shape, BF16),
            jax.ShapeDtypeStruct(pair_shape, BF16),
            jax.ShapeDtypeStruct((m, WIDTH), BF16),
        ],
        compiler_params=_params(1),
        name="moba_proj",
    )(x2d, norm_g, w_bf16)


def _moba_attn_kernel(sfeat_ref, slopes_ref, q_ref, k_ref, v_ref, o_ref,
                      kmean_ref, kaug_ref, vt_ref, s_ref, acc_ref, *, nb, unroll, tiles):
    L = MOBA_BLOCK
    p = pl.program_id(1)
    n = pl.program_id(2)
    lane = _iota((L, LANES), 1)
    head_lo = lane < HEAD_DIM
    head_masks = (head_lo, jnp.logical_not(head_lo))
    feat_base = (HEAD_DIM, 0)

    @pl.when(n == 0)
    def _():
        offs = _iota((L, LANES), 0).astype(F32)
        ones_row = (_iota((V_ROWS - HEAD_DIM, L), 0) == 0).astype(BF16)
        for j in range(nb):
            rows = slice(j * L, (j + 1) * L)
            kf = k_ref[0, 0, rows, :].astype(F32)
            kmean_ref[j:j + 1, :] = jnp.mean(kf, axis=0, keepdims=True)
            for hh in range(2):
                fb = feat_base[hh]
                feat = jnp.where(jnp.logical_or(lane == fb, lane == fb + 1), offs, 0.0)
                kaug_ref[hh, rows, :] = jnp.where(head_masks[hh], kf, feat).astype(BF16)
            v_t = v_ref[0, 0, rows, :].astype(F32).T.astype(BF16)
            for hh in range(2):
                vt_ref[j, hh, :HEAD_DIM, :] = v_t[hh * HEAD_DIM:(hh + 1) * HEAD_DIM, :]
                vt_ref[j, hh, HEAD_DIM:, :] = ones_row

    n_first = n * tiles
    units = [(t, hh) for t in range(tiles) for hh in range(2)]
    blk = _iota((nb, L), 0)
    kmean = kmean_ref[...]
    key_i = _iota((L, L), 0)
    qry_i = _iota((L, L), 1)

    qa, bits = [], []
    for t, hh in units:
        h = 2 * p + hh
        qf = q_ref[0, 0, t * L:(t + 1) * L, :]
        gate = _mm3(kmean, jnp.where(head_masks[hh], qf, 0.0), "nt")
        gate = jnp.where(blk < n_first + t, gate, NEG_INF)
        chosen_bits = jnp.zeros((1, L), jnp.int32)
        for _ in range(MOBA_TOPK):
            mx = jnp.max(gate, axis=0, keepdims=True)
            first = jnp.min(jnp.where(gate == mx, blk, nb), axis=0, keepdims=True)
            valid = mx > NEG_INF
            chosen_bits = chosen_bits | jnp.where(valid, jnp.left_shift(1, first), 0)
            gate = jnp.where(jnp.logical_and(blk == first, valid), NEG_INF, gate)
        bits.append(chosen_bits)
        fb = feat_base[hh]
        feat = jnp.where(lane == fb, sfeat_ref[2 * h],
                         jnp.where(lane == fb + 1, sfeat_ref[2 * h + 1], 0.0))
        qa.append(jnp.where(head_masks[hh], qf * (HEAD_DIM ** -0.5 * LOG2E), feat).astype(BF16))
    qa_all = [_cat([qa[2 * t + hh] for t in range(tiles)], 0) for hh in range(2)]

    nu = len(units)

    def issue_scores(g, slot):
        off = pl.multiple_of(jnp.minimum(g * unroll, nb - unroll) * L, L)
        per_head = [_dot_nt(kaug_ref[hh, pl.ds(off, unroll * L), :], qa_all[hh])
                    for hh in range(2)]
        maxes = []
        for u in range(unroll):
            for w, (t, hh) in enumerate(units):
                s = per_head[hh][u * L:(u + 1) * L, t * L:(t + 1) * L]
                s_ref[slot, nu * u + w] = s
                maxes.append(jnp.max(s, axis=0, keepdims=True))
        return maxes

    def consume(g, slot, maxes, ms):
        ms = list(ms)
        for u in range(unroll):
            j = g * unroll + u
            jc = jnp.minimum(j, nb - 1)
            for w, (t, hh) in enumerate(units):
                tile = n_first + t
                c = ((j - tile) * L).astype(F32) * LOG2E * slopes_ref[2 * p + hh]
                chosen = jnp.logical_and((jnp.right_shift(bits[w], jc) & 1) == 1, j < tile)
                m_new = jnp.maximum(ms[w], jnp.where(chosen, maxes[nu * u + w] + c, NEG_INF))
                alpha = jnp.exp2(ms[w] - m_new)
                shift = jnp.where(chosen, m_new - c, POS_INF)
                pr = jnp.exp2(s_ref[slot, nu * u + w] - shift).astype(BF16)
                acc_ref[w] = alpha * acc_ref[w] + _dot(vt_ref[jc, hh], pr)
                ms[w] = m_new
        return ms

    own = [_dot_nt(kaug_ref[hh, pl.ds(pl.multiple_of((n_first + t) * L, L), L), :], qa[2 * t + hh])
           for t, hh in units]
    first_maxes = issue_scores(0, 0)
    ms = []
    for w, (t, hh) in enumerate(units):
        tile = n_first + t
        s = jnp.where(key_i <= qry_i, own[w], NEG_INF)
        m = jnp.max(s, axis=0, keepdims=True)
        pr = jnp.exp2(s - m).astype(BF16)
        acc_ref[w] = _dot(vt_ref[tile, hh], pr)
        ms.append(m)

    nm = nu * unroll

    def body(i, carry):
        maxes, ms = carry[:nm], carry[nm:]
        maxes_b = issue_scores(2 * i + 1, 1)
        ms = consume(2 * i, 0, maxes, ms)
        maxes_a = issue_scores(2 * i + 2, 0)
        ms = consume(2 * i + 1, 1, maxes_b, ms)
        return (*maxes_a, *ms)

    per_trip = 2 * unroll
    n_past = n_first + tiles - 1
    lax.fori_loop(0, (n_past + per_trip - 1) // per_trip, body, (*first_maxes, *ms))
    for t in range(tiles):
        acc0, acc1 = acc_ref[2 * t], acc_ref[2 * t + 1]
        o_t = _cat([acc0[:HEAD_DIM] / acc0[HEAD_DIM:HEAD_DIM + 1],
                    acc1[:HEAD_DIM] / acc1[HEAD_DIM:HEAD_DIM + 1]], 0)
        o_ref[0, 0, t * L:(t + 1) * L, :] = o_t.T.astype(o_ref.dtype)


def _moba_attn(sfeat, slopes, q, k, v, unroll, tiles):
    batch, _, seq, _ = q.shape
    nb = seq // MOBA_BLOCK
    assert nb % tiles == 0
    tile = pl.BlockSpec((1, 1, tiles * MOBA_BLOCK, LANES), lambda b, p, n: (b, p, n, 0))
    full = pl.BlockSpec((1, 1, seq, LANES), lambda b, p, n: (b, p, 0, 0))
    smem = pl.BlockSpec(memory_space=pltpu.SMEM)
    return pl.pallas_call(
        functools.partial(_moba_attn_kernel, nb=nb, unroll=unroll, tiles=tiles),
        grid=(batch, N_PAIRS, nb // tiles),
        in_specs=[smem, smem, tile, full, full],
        out_specs=tile,
        out_shape=jax.ShapeDtypeStruct(q.shape, BF16),
        scratch_shapes=[pltpu.VMEM((nb, LANES), F32),
                        pltpu.VMEM((2, seq, LANES), BF16),
                        pltpu.VMEM((nb, 2, V_ROWS, MOBA_BLOCK), BF16),
                        pltpu.VMEM((2, 2 * tiles * unroll, MOBA_BLOCK, MOBA_BLOCK), F32),
                        pltpu.VMEM((2 * tiles, V_ROWS, MOBA_BLOCK), F32)],
        compiler_params=_params(3),
        name="moba_attn",
    )(sfeat, slopes, q, k, v)


def _out_proj_kernel(o_ref, sg_ref, x_ref, w_ref, g_ref, y_ref, *, final_norm):
    o = _cat([o_ref[0, p] for p in range(N_PAIRS)], 1)
    gated = (o.astype(F32) * sg_ref[...].astype(F32)).astype(BF16)
    y = x_ref[...] + _dot(gated, w_ref[...])
    if final_norm:
        y = _rms(y, g_ref[...])
    y_ref[...] = y


def _out_proj(o_pairs, sg, x2d, w_bf16, norm_g, seq, tm, final_norm):
    m = x2d.shape[0]
    nsb = seq // tm
    row_spec = pl.BlockSpec((tm, D_MODEL), lambda i: (i, 0))
    return pl.pallas_call(
        functools.partial(_out_proj_kernel, final_norm=final_norm),
        grid=(m // tm,),
        in_specs=[
            pl.BlockSpec((1, N_PAIRS, tm, LANES), lambda i: (i // nsb, 0, i % nsb, 0)),
            row_spec, row_spec,
            _const_spec((WIDTH, D_MODEL)),
            _const_spec((1, D_MODEL)),
        ],
        out_specs=row_spec,
        out_shape=jax.ShapeDtypeStruct((m, D_MODEL), F32),
        compiler_params=_params(1),
        name="out_proj_final" if final_norm else "out_proj",
    )(o_pairs, sg, x2d, w_bf16, norm_g)


def _rwkv_proj_kernel(x_ref, halo_ref, g_ref, mix_ref, win_ref, w0_ref, w1t_ref, w2_ref,
                      a0_ref, a1t_ref, a2_ref, kk_ref, ka_ref, rk_ref,
                      r_out, k_out, v_out, lw_out, kk_out, bb_out, bonus_out, sg_out,
                      *, tiles_per_seq):
    i = pl.program_id(0)
    tm = x_ref.shape[0]
    g = g_ref[...]
    h = _rms(x_ref[...], g)
    prev_row = _rms(halo_ref[...], g)[7:8, :]
    prev_row = jnp.where(i % tiles_per_seq == 0, 0.0, prev_row)
    rolled = pltpu.roll(h, 1, 0)
    h_prev = jnp.where(_iota((tm, D_MODEL), 0) == 0, prev_row, rolled)
    xx = h_prev - h

    def stream(n):
        return h + xx * mix_ref[n:n + 1, :]

    r = _dot(stream(0).astype(BF16), win_ref[0])
    k = _dot(stream(1).astype(BF16), win_ref[1])
    v = _dot(stream(2).astype(BF16), win_ref[2])
    gt = _dot(stream(3).astype(BF16), win_ref[3])
    sg_out[...] = (gt * jax.nn.sigmoid(gt)).astype(BF16)

    def lora(xs, down_t_ref, up_ref, act):
        mid_t = act(_dot_nt(down_t_ref[...], xs.astype(BF16)))
        return _dot_tn(mid_t.astype(BF16), up_ref[...])

    z = -(w0_ref[...] + lora(stream(4), w1t_ref, w2_ref, jnp.tanh))
    softplus = jnp.maximum(z, 0.0) + jnp.log(1.0 + jnp.exp(-jnp.abs(z)))
    lw = -jnp.exp(-softplus - 0.5)
    a = jax.nn.sigmoid(a0_ref[...] + lora(stream(5), a1t_ref, a2_ref, lambda t: t))
    kr = k * kk_ref[...]
    k_mod = k * (1.0 + (a - 1.0) * ka_ref[...])
    rk = r * k_mod * rk_ref[...]
    ones_bd = _head_blockdiag(LANES).astype(BF16)
    ones_bd2 = _cat([ones_bd, ones_bd], 0)
    for p in range(N_PAIRS):
        cols = slice(p * LANES, (p + 1) * LANES)
        kr_p = kr[:, cols]
        ss = _mm2(kr_p * kr_p, ones_bd2)
        kk_p = kr_p * jnp.minimum(lax.rsqrt(ss), 1e12)
        r_out[0, p] = r[:, cols]
        k_out[0, p] = k_mod[:, cols]
        v_out[0, p] = v[:, cols]
        lw_out[0, p] = lw[:, cols]
        kk_out[0, p] = kk_p
        bb_out[0, p] = kk_p * a[:, cols]
        bonus_out[0, p] = _mm2(rk[:, cols], ones_bd2) * v[:, cols]


def _rwkv_proj(x2d, weights, batch, seq, tm):
    m = x2d.shape[0]
    nsb = seq // tm
    halo_blocks = tm // 8
    pair_spec = pl.BlockSpec((1, N_PAIRS, tm, LANES), lambda i: (i // nsb, 0, i % nsb, 0))
    pair_shape = jax.ShapeDtypeStruct((batch, N_PAIRS, seq, LANES), F32)
    vec = _const_spec((1, WIDTH))
    lora_w = _const_spec((LORA, WIDTH))
    return pl.pallas_call(
        functools.partial(_rwkv_proj_kernel, tiles_per_seq=nsb),
        grid=(m // tm,),
        in_specs=[
            pl.BlockSpec((tm, D_MODEL), lambda i: (i, 0)),
            pl.BlockSpec((8, D_MODEL), lambda i: (jnp.maximum(i * halo_blocks - 1, 0), 0)),
            _const_spec((1, D_MODEL)),
            _const_spec((6, D_MODEL)),
            _const_spec((4, D_MODEL, WIDTH)),
            vec, lora_w, lora_w,
            vec, lora_w, lora_w,
            vec, vec, vec,
        ],
        out_specs=[pair_spec] * 7 + [pl.BlockSpec((tm, WIDTH), lambda i: (i, 0))],
        out_shape=[pair_shape] * 7 + [jax.ShapeDtypeStruct((m, WIDTH), BF16)],
        compiler_params=_params(1),
        name="rwkv_proj",
    )(x2d, x2d, *weights)


def _rwkv_scan_kernel(r_ref, k_ref, v_ref, lw_ref, kk_ref, bb_ref, bonus_ref, lng_ref, lnb_ref,
                      o_ref, z_ref, *, chunks_per_step):
    C = CHUNK
    pairs = range(N_PAIRS)

    @pl.when(pl.program_id(1) == 0)
    def _():
        z_ref[...] = jnp.zeros_like(z_ref)

    row = _iota((C, LANES), 0)
    lane = _iota((C, LANES), 1)
    col = lane % C
    head_lo = lane < HEAD_DIM
    keep_lo = head_lo.astype(BF16)
    keep_hi = jnp.logical_not(head_lo).astype(BF16)
    strict = col < row
    row4 = _iota((C, 2 * LANES), 0)
    lane4 = _iota((C, 2 * LANES), 1)
    col4 = lane4 % C
    lower2 = col4 <= row4
    eye = (row4 == col4).astype(F32)
    same_block = {b: (row4 // b) == (col4 // b) for b in (8, 16, 32)}
    keep_head = [(lane4 // HEAD_DIM == h).astype(BF16) for h in range(4)]
    tri3 = ((_iota((C, 3 * C), 1) % C) <= _iota((C, 3 * C), 0)).astype(BF16)
    head_bd = _head_blockdiag(LANES)
    avg_bd = head_bd.astype(BF16) * (1.0 / HEAD_DIM)
    avg_bd2 = _cat([avg_bd, avg_bd], 0)

    def blockdiag(xb):
        return _cat([xb * keep_lo, xb * keep_hi], 0)

    def prod(a_list, b_list):
        parts = [(_split2(a), b.astype(BF16)) for a, b in zip(a_list, b_list)]
        return [_mm_left(ah, al, blockdiag(bb), "nn") for (ah, al), bb in parts]

    def prod_bf16(a_list, b_list):
        parts = [(a.astype(BF16), b.astype(BF16)) for a, b in zip(a_list, b_list)]
        return [_dot(ab, _cat([bb * keep for keep in keep_head], 0)) for ab, bb in parts]

    def state_free(c, out):
        rows = slice(c * C, (c + 1) * C)
        v = [v_ref[0, p, rows, :] for p in pairs]
        lw = [lw_ref[0, p, rows, :] for p in pairs]
        lw3 = [_cat(_split3(x), 0) for x in lw]
        cs = [_dot(tri3, x) for x in lw3]
        yield
        decay_in = [jnp.exp(x) for x in cs]
        inv = [jnp.exp(-x) for x in cs]
        r_t = [r_ref[0, p, rows, :] * decay_in[p] for p in pairs]
        k_t = [k_ref[0, p, rows, :] * inv[p] for p in pairs]
        b_t = [bb_ref[0, p, rows, :] * inv[p] for p in pairs]
        a_t = [-(kk_ref[0, p, rows, :] * jnp.exp(cs[p] - lw[p])) for p in pairs]
        ar = [_split2(_cat([a_t[p], r_t[p]], 0)) for p in pairs]
        bk_rows = [_cat([blockdiag(b_t[p].astype(BF16)), blockdiag(k_t[p].astype(BF16))], 0)
                   for p in pairs]
        mbk = [_mm_left(*ar[p], bk_rows[p], "nt") for p in pairs]
        yield
        low = [jnp.where(strict, x[:C, :LANES], 0.0) for x in mbk]
        ak = [jnp.where(strict, x[:C, LANES:], 0.0) for x in mbk]
        rbk = [jnp.where(lower2, x[C:], 0.0) for x in mbk]

        quads = range(N_PAIRS // 2)
        low4 = [_cat([low[2 * i], low[2 * i + 1]], 1) for i in quads]
        ld = [jnp.where(same_block[8], x, 0.0) for x in low4]
        l2 = prod_bf16(ld, ld)
        yield
        l4 = prod_bf16(l2, l2)
        l3 = prod_bf16(ld, l2)
        yield
        p1 = [eye + ld[i] + l2[i] + l3[i] for i in quads]
        p1l4 = prod_bf16(p1, l4)
        yield
        t4 = [p1[i] + p1l4[i] for i in quads]
        b = 8
        while b < C:
            couple = jnp.logical_not(same_block[b])
            if 2 * b < C:
                couple = jnp.logical_and(same_block[2 * b], couple)
            x = [jnp.where(couple, y, 0.0) for y in low4]
            tx = prod_bf16(t4, x)
            yield
            txt = prod_bf16(tx, t4)
            yield
            t4 = [t4[i] + txt[i] for i in quads]
            b *= 2
        t = [t4[p // 2][:, (p % 2) * LANES:(p % 2 + 1) * LANES] for p in pairs]
        akv = prod(ak, v)
        yield
        scale = [x[C - 1:C, :] for x in decay_in]
        out.update(
            v=v, ar=ar, rsp=[_split2(x) for x in rbk], t=t, akv=akv,
            bk=[_split2(_cat([b_t[p] * scale[p], k_t[p] * scale[p]], 0)) for p in pairs],
            decay_cols=[x[C - 8:C, :].T[:, 7:8] for x in decay_in])

    def state_step(c, pre, z):
        rows = slice(c * C, (c + 1) * C)
        st = [_mm_left(*pre["ar"][p], z[p].astype(BF16), "nn") for p in pairs]
        yield
        rhs = [st[p][:C] + pre["akv"][p] for p in pairs]
        u = prod(pre["t"], rhs)
        yield
        ub = [x.astype(BF16) for x in u]
        vb = [x.astype(BF16) for x in pre["v"]]
        uv = [_cat([blockdiag(ub[p]), blockdiag(vb[p])], 0) for p in pairs]
        y = [st[p][C:] + _mm_left(*pre["rsp"][p], uv[p], "nn") for p in pairs]
        upd = [_mm_left(*pre["bk"][p], _cat([ub[p], vb[p]], 0), "tn") for p in pairs]
        yield
        for p in pairs:
            z[p] = pre["decay_cols"][p] * z[p] + jnp.where(head_bd, upd[p], 0.0)
        mu = [_mm2(x, avg_bd2) for x in y]
        yield
        d = [y[p] - mu[p] for p in pairs]
        var = [_mm2(x * x, avg_bd2) for x in d]
        yield
        for p in pairs:
            yn = d[p] * lax.rsqrt(var[p] + LNX_EPS) * lng_ref[p] + lnb_ref[p]
            o_ref[0, p, rows, :] = (yn + bonus_ref[0, p, rows, :]).astype(o_ref.dtype)

    z = [z_ref[p] for p in pairs]
    prepared, running = {}, {}

    def start_more():
        while len(running) < SCAN_CHUNKS_AHEAD and len(prepared) < chunks_per_step:
            c_new = len(prepared)
            prepared[c_new] = {}
            running[c_new] = state_free(c_new, prepared[c_new])

    def advance():
        for c_run in list(running):
            if next(running[c_run], "done") == "done":
                del running[c_run]
        start_more()

    start_more()
    for c in range(chunks_per_step):
        while c in running:
            advance()
        for _ in state_step(c, prepared[c], z):
            advance()
    for p in pairs:
        z_ref[p] = z[p]


def _rwkv_scan(r, k, v, lw, kk, bb, bonus, lnx_g, lnx_b, chunks_per_step):
    batch, _, seq, _ = r.shape
    tc = CHUNK * chunks_per_step
    blk = pl.BlockSpec((1, N_PAIRS, tc, LANES), lambda b, c: (b, 0, c, 0))
    vec = _const_spec((N_PAIRS, 1, LANES))
    return pl.pallas_call(
        functools.partial(_rwkv_scan_kernel, chunks_per_step=chunks_per_step),
        grid=(batch, seq // tc),
        in_specs=[blk] * 7 + [vec, vec],
        out_specs=blk,
        out_shape=jax.ShapeDtypeStruct(r.shape, BF16),
        scratch_shapes=[pltpu.VMEM((N_PAIRS, LANES, LANES), F32)],
        compiler_params=_params(2),
        name="rwkv_scan",
    )(r, k, v, lw, kk, bb, bonus, lnx_g, lnx_b)


def kernel(x, moba_norm_g, moba_w_in, moba_w_out, rwkv_norm_g, rwkv_mix, rwkv_w_in, rwkv_w0, rwkv_w1, rwkv_w2, rwkv_a0, rwkv_a1, rwkv_a2, rwkv_k_k, rwkv_k_a, rwkv_r_k, rwkv_lnx_g, rwkv_lnx_b, rwkv_w_out, final_norm_g):
    batch, seq, d = x.shape
    assert d == D_MODEL and seq % MOBA_BLOCK == 0 and seq // MOBA_BLOCK <= 32
    x2d = x.reshape(batch * seq, d)
    row = lambda t: t.reshape(1, -1).astype(F32)

    q, k, v, sg = _moba_proj(x2d, row(moba_norm_g), moba_w_in.astype(BF16), batch, seq, tm=512)
    slopes = jnp.asarray([2.0 ** (-8.0 * (i + 1) / N_HEADS) for i in range(N_HEADS)], F32)
    slope_hi = (slopes * LOG2E).astype(BF16).astype(F32)
    sfeat = jnp.stack([slope_hi, slopes * LOG2E - slope_hi], axis=1).reshape(-1)
    o = _moba_attn(sfeat, slopes, q, k, v, unroll=2, tiles=4)
    x1 = _out_proj(o, sg, x2d, moba_w_out.astype(BF16), row(final_norm_g), seq, tm=512,
                   final_norm=False)

    weights = (row(rwkv_norm_g), rwkv_mix.astype(F32), rwkv_w_in.astype(BF16),
               row(rwkv_w0), rwkv_w1.T.astype(BF16), rwkv_w2.astype(BF16),
               row(rwkv_a0), rwkv_a1.T.astype(BF16), rwkv_a2.astype(BF16),
               row(rwkv_k_k), row(rwkv_k_a), row(rwkv_r_k))
    r, k2, v2, lw, kk, bb, bonus, sg2 = _rwkv_proj(x1, weights, batch, seq, tm=256)
    pair_vec = lambda t: t.reshape(N_PAIRS, 1, LANES).astype(F32)
    o2 = _rwkv_scan(r, k2, v2, lw, kk, bb, bonus, pair_vec(rwkv_lnx_g), pair_vec(rwkv_lnx_b),
                    chunks_per_step=8)
    out = _out_proj(o2, sg2, x1, rwkv_w_out.astype(BF16), row(final_norm_g), seq, tm=512,
                    final_norm=True)
    return out.reshape(batch, seq, d)
```

```python
import functools
import math

import jax
import jax.numpy as jnp
from jax import lax
from jax.experimental import pallas as pl
from jax.experimental.pallas import tpu as pltpu

F32 = jnp.float32
BF16 = jnp.bfloat16

D_MODEL = 1024
N_HEADS = 16
HEAD_DIM = 64
WIDTH = N_HEADS * HEAD_DIM
LANES = 128
N_PAIRS = WIDTH // LANES
MOBA_BLOCK = 256
MOBA_TOPK = 3
BF16_SUBLANES = 16
V_ROWS = HEAD_DIM + BF16_SUBLANES
LORA = 64
NORM_EPS = 1e-6
LNX_EPS = 64e-5
CHUNK = 64
SCAN_CHUNKS_AHEAD = 3
VMEM_LIMIT = 56 * 1024 * 1024
NEG_INF = float("-inf")
POS_INF = float("inf")
LOG2E = math.log2(math.e)


def _dot(a, b):
    return jnp.dot(a, b, preferred_element_type=F32)


def _dot_nt(a, b):
    return lax.dot_general(a, b, (((1,), (1,)), ((), ())), preferred_element_type=F32)


def _dot_tn(a, b):
    return lax.dot_general(a, b, (((0,), (0,)), ((), ())), preferred_element_type=F32)


def _split2(x):
    hi = x.astype(BF16)
    lo = (x - hi.astype(F32)).astype(BF16)
    return hi, lo


def _split3(x):
    hi = x.astype(BF16)
    r1 = x - hi.astype(F32)
    mid = r1.astype(BF16)
    lo = (r1 - mid.astype(F32)).astype(BF16)
    return hi, mid, lo


def _cat(xs, axis):
    return jnp.concatenate(xs, axis=axis)


def _mm3_parts(ah, al, bh, bl, kind):
    if kind == "nn":
        return _dot(_cat([ah, al], 1), _cat([bh, bh], 0)) + _dot(ah, bl)
    if kind == "nt":
        return _dot_nt(_cat([ah, al], 1), _cat([bh, bh], 1)) + _dot_nt(ah, bl)
    assert kind == "tn"
    return _dot_tn(_cat([ah, al], 0), _cat([bh, bh], 0)) + _dot_tn(ah, bl)


def _mm_left(ah, al, b, kind):
    if kind == "nn":
        return _dot(_cat([ah, al], 1), _cat([b, b], 0))
    if kind == "nt":
        return _dot_nt(_cat([ah, al], 1), _cat([b, b], 1))
    assert kind == "tn"
    return _dot_tn(_cat([ah, al], 0), _cat([b, b], 0))


def _mm3(a, b, kind="nn"):
    ah, al = _split2(a)
    bh, bl = _split2(b)
    return _mm3_parts(ah, al, bh, bl, kind)


def _mm2(a, b2_exact):
    ah, al = _split2(a)
    return _dot(_cat([ah, al], 1), b2_exact)


def _rms(x, g):
    ms = jnp.mean(x * x, axis=-1, keepdims=True)
    return x * lax.rsqrt(ms + NORM_EPS) * g


def _iota(shape, dim):
    return lax.broadcasted_iota(jnp.int32, shape, dim)


def _head_blockdiag(n):
    return (_iota((n, n), 0) // HEAD_DIM) == (_iota((n, n), 1) // HEAD_DIM)


def _const_spec(shape):
    nd = len(shape)
    return pl.BlockSpec(shape, lambda *_: (0,) * nd)


def _params(n_axes):
    return pltpu.CompilerParams(dimension_semantics=("arbitrary",) * n_axes,
                                vmem_limit_bytes=VMEM_LIMIT)


def _moba_proj_kernel(x_ref, g_ref, w_ref, q_ref, k_ref, v_ref, sg_ref):
    h = _rms(x_ref[...], g_ref[...]).astype(BF16)
    q = _dot(h, w_ref[:, 0 * WIDTH:1 * WIDTH])
    k = _dot(h, w_ref[:, 1 * WIDTH:2 * WIDTH])
    v = _dot(h, w_ref[:, 2 * WIDTH:3 * WIDTH])
    gate = _dot(h, w_ref[:, 3 * WIDTH:4 * WIDTH])
    for p in range(N_PAIRS):
        cols = slice(p * LANES, (p + 1) * LANES)
        q_ref[0, p] = q[:, cols]
        k_ref[0, p] = k[:, cols].astype(BF16)
        v_ref[0, p] = v[:, cols].astype(BF16)
    sg_ref[...] = (gate * jax.nn.sigmoid(gate)).astype(BF16)


def _moba_proj(x2d, norm_g, w_bf16, batch, seq, tm):
    m = x2d.shape[0]
    nsb = seq // tm
    pair_spec = pl.BlockSpec((1, N_PAIRS, tm, LANES), lambda i: (i // nsb, 0, i % nsb, 0))
    pair_shape = (batch, N_PAIRS, seq, LANES)
    return pl.pallas_call(
        _moba_proj_kernel,
        grid=(m // tm,),
        in_specs=[
            pl.BlockSpec((tm, D_MODEL), lambda i: (i, 0)),
            _const_spec((1, D_MODEL)),
            _const_spec((D_MODEL, 4 * WIDTH)),
        ],
        out_specs=[pair_spec, pair_spec, pair_spec,
                   pl.BlockSpec((tm, WIDTH), lambda i: (i, 0))],
        out_shape=[
            jax.ShapeDtypeStruct(pair_shape, F32),
            jax.ShapeDtypeStruct(pair_shape, BF16),
            jax.ShapeDtypeStruct(pair_shape, BF16),
            jax.ShapeDtypeStruct((m, WIDTH), BF16),
        ],
        compiler_params=_params(1),
        name="moba_proj",
    )(x2d, norm_g, w_bf16)


def _moba_attn_kernel(sfeat_ref, slopes_ref, q_ref, k_ref, v_ref, o_ref,
                      kmean_ref, kaug_ref, vt_ref, s_ref, acc_ref, *, nb, unroll, tiles):
    L = MOBA_BLOCK
    p = pl.program_id(1)
    n = pl.program_id(2)
    lane = _iota((L, LANES), 1)
    head_lo = lane < HEAD_DIM
    head_masks = (head_lo, jnp.logical_not(head_lo))
    feat_base = (HEAD_DIM, 0)

    @pl.when(n == 0)
    def _():
        offs = _iota((L, LANES), 0).astype(F32)
        ones_row = (_iota((V_ROWS - HEAD_DIM, L), 0) == 0).astype(BF16)
        for j in range(nb):
            rows = slice(j * L, (j + 1) * L)
            kf = k_ref[0, 0, rows, :].astype(F32)
            kmean_ref[j:j + 1, :] = jnp.mean(kf, axis=0, keepdims=True)
            for hh in range(2):
                fb = feat_base[hh]
                feat = jnp.where(jnp.logical_or(lane == fb, lane == fb + 1), offs, 0.0)
                kaug_ref[hh, rows, :] = jnp.where(head_masks[hh], kf, feat).astype(BF16)
            v_t = v_ref[0, 0, rows, :].astype(F32).T.astype(BF16)
            for hh in range(2):
                vt_ref[j, hh, :HEAD_DIM, :] = v_t[hh * HEAD_DIM:(hh + 1) * HEAD_DIM, :]
                vt_ref[j, hh, HEAD_DIM:, :] = ones_row

    n_first = n * tiles
    units = [(t, hh) for t in range(tiles) for hh in range(2)]
    blk = _iota((nb, L), 0)
    kmean = kmean_ref[...]
    key_i = _iota((L, L), 0)
    qry_i = _iota((L, L), 1)

    qa, bits = [], []
    for t, hh in units:
        h = 2 * p + hh
        qf = q_ref[0, 0, t * L:(t + 1) * L, :]
        gate = _mm3(kmean, jnp.where(head_masks[hh], qf, 0.0), "nt")
        gate = jnp.where(blk < n_first + t, gate, NEG_INF)
        chosen_bits = jnp.zeros((1, L), jnp.int32)
        for _ in range(MOBA_TOPK):
            mx = jnp.max(gate, axis=0, keepdims=True)
            first = jnp.min(jnp.where(gate == mx, blk, nb), axis=0, keepdims=True)
            valid = mx > NEG_INF
            chosen_bits = chosen_bits | jnp.where(valid, jnp.left_shift(1, first), 0)
            gate = jnp.where(jnp.logical_and(blk == first, valid), NEG_INF, gate)
        bits.append(chosen_bits)
        fb = feat_base[hh]
        feat = jnp.where(lane == fb, sfeat_ref[2 * h],
                         jnp.where(lane == fb + 1, sfeat_ref[2 * h + 1], 0.0))
        qa.append(jnp.where(head_masks[hh], qf * (HEAD_DIM ** -0.5 * LOG2E), feat).astype(BF16))
    qa_all = [_cat([qa[2 * t + hh] for t in range(tiles)], 0) for hh in range(2)]

    nu = len(units)

    def issue_scores(g, slot):
        off = pl.multiple_of(jnp.minimum(g * unroll, nb - unroll) * L, L)
        per_head = [_dot_nt(kaug_ref[hh, pl.ds(off, unroll * L), :], qa_all[hh])
                    for hh in range(2)]
        maxes = []
        for u in range(unroll):
            for w, (t, hh) in enumerate(units):
                s = per_head[hh][u * L:(u + 1) * L, t * L:(t + 1) * L]
                s_ref[slot, nu * u + w] = s
                maxes.append(jnp.max(s, axis=0, keepdims=True))
        return maxes

    def consume(g, slot, maxes, ms):
        ms = list(ms)
        for u in range(unroll):
            j = g * unroll + u
            jc = jnp.minimum(j, nb - 1)
            for w, (t, hh) in enumerate(units):
                tile = n_first + t
                c = ((j - tile) * L).astype(F32) * LOG2E * slopes_ref[2 * p + hh]
                chosen = jnp.logical_and((jnp.right_shift(bits[w], jc) & 1) == 1, j < tile)
                m_new = jnp.maximum(ms[w], jnp.where(chosen, maxes[nu * u + w] + c, NEG_INF))
                alpha = jnp.exp2(ms[w] - m_new)
                shift = jnp.where(chosen, m_new - c, POS_INF)
                pr = jnp.exp2(s_ref[slot, nu * u + w] - shift).astype(BF16)
                acc_ref[w] = alpha * acc_ref[w] + _dot(vt_ref[jc, hh], pr)
                ms[w] = m_new
        return ms

    own = [_dot_nt(kaug_ref[hh, pl.ds(pl.multiple_of((n_first + t) * L, L), L), :], qa[2 * t + hh])
           for t, hh in units]
    first_maxes = issue_scores(0, 0)
    ms = []
    for w, (t, hh) in enumerate(units):
        tile = n_first + t
        s = jnp.where(key_i <= qry_i, own[w], NEG_INF)
        m = jnp.max(s, axis=0, keepdims=True)
        pr = jnp.exp2(s - m).astype(BF16)
        acc_ref[w] = _dot(vt_ref[tile, hh], pr)
        ms.append(m)

    nm = nu * unroll

    def body(i, carry):
        maxes, ms = carry[:nm], carry[nm:]
        maxes_b = issue_scores(2 * i + 1, 1)
        ms = consume(2 * i, 0, maxes, ms)
        maxes_a = issue_scores(2 * i + 2, 0)
        ms = consume(2 * i + 1, 1, maxes_b, ms)
        return (*maxes_a, *ms)

    per_trip = 2 * unroll
    n_past = n_first + tiles - 1
    lax.fori_loop(0, (n_past + per_trip - 1) // per_trip, body, (*first_maxes, *ms))
    for t in range(tiles):
        acc0, acc1 = acc_ref[2 * t], acc_ref[2 * t + 1]
        o_t = _cat([acc0[:HEAD_DIM] / acc0[HEAD_DIM:HEAD_DIM + 1],
                    acc1[:HEAD_DIM] / acc1[HEAD_DIM:HEAD_DIM + 1]], 0)
        o_ref[0, 0, t * L:(t + 1) * L, :] = o_t.T.astype(o_ref.dtype)


def _moba_attn(sfeat, slopes, q, k, v, unroll, tiles):
    batch, _, seq, _ = q.shape
    nb = seq // MOBA_BLOCK
    assert nb % tiles == 0
    tile = pl.BlockSpec((1, 1, tiles * MOBA_BLOCK, LANES), lambda b, p, n: (b, p, n, 0))
    full = pl.BlockSpec((1, 1, seq, LANES), lambda b, p, n: (b, p, 0, 0))
    smem = pl.BlockSpec(memory_space=pltpu.SMEM)
    return pl.pallas_call(
        functools.partial(_moba_attn_kernel, nb=nb, unroll=unroll, tiles=tiles),
        grid=(batch, N_PAIRS, nb // tiles),
        in_specs=[smem, smem, tile, full, full],
        out_specs=tile,
        out_shape=jax.ShapeDtypeStruct(q.shape, BF16),
        scratch_shapes=[pltpu.VMEM((nb, LANES), F32),
                        pltpu.VMEM((2, seq, LANES), BF16),
                        pltpu.VMEM((nb, 2, V_ROWS, MOBA_BLOCK), BF16),
                        pltpu.VMEM((2, 2 * tiles * unroll, MOBA_BLOCK, MOBA_BLOCK), F32),
                        pltpu.VMEM((2 * tiles, V_ROWS, MOBA_BLOCK), F32)],
        compiler_params=_params(3),
        name="moba_attn",
    )(sfeat, slopes, q, k, v)


def _out_proj_kernel(o_ref, sg_ref, x_ref, w_ref, g_ref, y_ref, *, final_norm):
    o = _cat([o_ref[0, p] for p in range(N_PAIRS)], 1)
    gated = (o.astype(F32) * sg_ref[...].astype(F32)).astype(BF16)
    y = x_ref[...] + _dot(gated, w_ref[...])
    if final_norm:
        y = _rms(y, g_ref[...])
    y_ref[...] = y


def _out_proj(o_pairs, sg, x2d, w_bf16, norm_g, seq, tm, final_norm):
    m = x2d.shape[0]
    nsb = seq // tm
    row_spec = pl.BlockSpec((tm, D_MODEL), lambda i: (i, 0))
    return pl.pallas_call(
        functools.partial(_out_proj_kernel, final_norm=final_norm),
        grid=(m // tm,),
        in_specs=[
            pl.BlockSpec((1, N_PAIRS, tm, LANES), lambda i: (i // nsb, 0, i % nsb, 0)),
            row_spec, row_spec,
            _const_spec((WIDTH, D_MODEL)),
            _const_spec((1, D_MODEL)),
        ],
        out_specs=row_spec,
        out_shape=jax.ShapeDtypeStruct((m, D_MODEL), F32),
        compiler_params=_params(1),
        name="out_proj_final" if final_norm else "out_proj",
    )(o_pairs, sg, x2d, w_bf16, norm_g)


def _rwkv_proj_kernel(x_ref, halo_ref, g_ref, mix_ref, win_ref, w0_ref, w1t_ref, w2_ref,
                      a0_ref, a1t_ref, a2_ref, kk_ref, ka_ref, rk_ref,
                      r_out, k_out, v_out, lw_out, kk_out, bb_out, bonus_out, sg_out,
                      *, tiles_per_seq):
    i = pl.program_id(0)
    tm = x_ref.shape[0]
    g = g_ref[...]
    h = _rms(x_ref[...], g)
    prev_row = _rms(halo_ref[...], g)[7:8, :]
    prev_row = jnp.where(i % tiles_per_seq == 0, 0.0, prev_row)
    rolled = pltpu.roll(h, 1, 0)
    h_prev = jnp.where(_iota((tm, D_MODEL), 0) == 0, prev_row, rolled)
    xx = h_prev - h

    def stream(n):
        return h + xx * mix_ref[n:n + 1, :]

    r = _dot(stream(0).astype(BF16), win_ref[0])
    k = _dot(stream(1).astype(BF16), win_ref[1])
    v = _dot(stream(2).astype(BF16), win_ref[2])
    gt = _dot(stream(3).astype(BF16), win_ref[3])
    sg_out[...] = (gt * jax.nn.sigmoid(gt)).astype(BF16)

    def lora(xs, down_t_ref, up_ref, act):
        mid_t = act(_dot_nt(down_t_ref[...], xs.astype(BF16)))
        return _dot_tn(mid_t.astype(BF16), up_ref[...])

    z = -(w0_ref[...] + lora(stream(4), w1t_ref, w2_ref, jnp.tanh))
    softplus = jnp.maximum(z, 0.0) + jnp.log(1.0 + jnp.exp(-jnp.abs(z)))
    lw = -jnp.exp(-softplus - 0.5)
    a = jax.nn.sigmoid(a0_ref[...] + lora(stream(5), a1t_ref, a2_ref, lambda t: t))
    kr = k * kk_ref[...]
    k_mod = k * (1.0 + (a - 1.0) * ka_ref[...])
    rk = r * k_mod * rk_ref[...]
    ones_bd = _head_blockdiag(LANES).astype(BF16)
    for p in range(N_PAIRS):
        cols = slice(p * LANES, (p + 1) * LANES)
        kr_p = kr[:, cols]
        ss = _dot((kr_p * kr_p).astype(BF16), ones_bd)
        kk_p = kr_p * jnp.minimum(lax.rsqrt(ss), 1e12)
        r_out[0, p] = r[:, cols]
        k_out[0, p] = k_mod[:, cols]
        v_out[0, p] = v[:, cols]
        lw_out[0, p] = lw[:, cols]
        kk_out[0, p] = kk_p
        bb_out[0, p] = kk_p * a[:, cols]
        bonus_out[0, p] = _dot(rk[:, cols].astype(BF16), ones_bd) * v[:, cols]


def _rwkv_proj(x2d, weights, batch, seq, tm):
    m = x2d.shape[0]
    nsb = seq // tm
    halo_blocks = tm // 8
    pair_spec = pl.BlockSpec((1, N_PAIRS, tm, LANES), lambda i: (i // nsb, 0, i % nsb, 0))
    pair_shape = jax.ShapeDtypeStruct((batch, N_PAIRS, seq, LANES), F32)
    vec = _const_spec((1, WIDTH))
    lora_w = _const_spec((LORA, WIDTH))
    return pl.pallas_call(
        functools.partial(_rwkv_proj_kernel, tiles_per_seq=nsb),
        grid=(m // tm,),
        in_specs=[
            pl.BlockSpec((tm, D_MODEL), lambda i: (i, 0)),
            pl.BlockSpec((8, D_MODEL), lambda i: (jnp.maximum(i * halo_blocks - 1, 0), 0)),
            _const_spec((1, D_MODEL)),
            _const_spec((6, D_MODEL)),
            _const_spec((4, D_MODEL, WIDTH)),
            vec, lora_w, lora_w,
            vec, lora_w, lora_w,
            vec, vec, vec,
        ],
        out_specs=[pair_spec] * 7 + [pl.BlockSpec((tm, WIDTH), lambda i: (i, 0))],
        out_shape=[pair_shape] * 7 + [jax.ShapeDtypeStruct((m, WIDTH), BF16)],
        compiler_params=_params(1),
        name="rwkv_proj",
    )(x2d, x2d, *weights)


def _rwkv_scan_kernel(r_ref, k_ref, v_ref, lw_ref, kk_ref, bb_ref, bonus_ref, lng_ref, lnb_ref,
                      o_ref, z_ref, *, chunks_per_step):
    C = CHUNK
    pairs = range(N_PAIRS)

    @pl.when(pl.program_id(1) == 0)
    def _():
        z_ref[...] = jnp.zeros_like(z_ref)

    row = _iota((C, LANES), 0)
    lane = _iota((C, LANES), 1)
    col = lane % C
    head_lo = lane < HEAD_DIM
    keep_lo = head_lo.astype(BF16)
    keep_hi = jnp.logical_not(head_lo).astype(BF16)
    strict = col < row
    row4 = _iota((C, 2 * LANES), 0)
    lane4 = _iota((C, 2 * LANES), 1)
    col4 = lane4 % C
    lower2 = col4 <= row4
    eye = (row4 == col4).astype(F32)
    same_block = {b: (row4 // b) == (col4 // b) for b in (8, 16, 32)}
    keep_head = [(lane4 // HEAD_DIM == h).astype(BF16) for h in range(4)]
    tri3 = ((_iota((C, 3 * C), 1) % C) <= _iota((C, 3 * C), 0)).astype(BF16)
    head_bd = _head_blockdiag(LANES)
    avg_bd = head_bd.astype(BF16) * (1.0 / HEAD_DIM)
    avg_bd2 = _cat([avg_bd, avg_bd], 0)

    def blockdiag(xb):
        return _cat([xb * keep_lo, xb * keep_hi], 0)

    def prod(a_list, b_list):
        parts = [(_split2(a), b.astype(BF16)) for a, b in zip(a_list, b_list)]
        return [_mm_left(ah, al, blockdiag(bb), "nn") for (ah, al), bb in parts]

    def prod_bf16(a_list, b_list):
        parts = [(a.astype(BF16), b.astype(BF16)) for a, b in zip(a_list, b_list)]
        return [_dot(ab, _cat([bb * keep for keep in keep_head], 0)) for ab, bb in parts]

    def state_free(c, out):
        rows = slice(c * C, (c + 1) * C)
        v = [v_ref[0, p, rows, :] for p in pairs]
        lw = [lw_ref[0, p, rows, :] for p in pairs]
        lw3 = [_cat(_split3(x), 0) for x in lw]
        cs = [_dot(tri3, x) for x in lw3]
        yield
        decay_in = [jnp.exp(x) for x in cs]
        inv = [jnp.exp(-x) for x in cs]
        r_t = [r_ref[0, p, rows, :] * decay_in[p] for p in pairs]
        k_t = [k_ref[0, p, rows, :] * inv[p] for p in pairs]
        b_t = [bb_ref[0, p, rows, :] * inv[p] for p in pairs]
        a_t = [-(kk_ref[0, p, rows, :] * jnp.exp(cs[p] - lw[p])) for p in pairs]
        ar = [_split2(_cat([a_t[p], r_t[p]], 0)) for p in pairs]
        bk_rows = [_cat([blockdiag(b_t[p].astype(BF16)), blockdiag(k_t[p].astype(BF16))], 0)
                   for p in pairs]
        mbk = [_mm_left(*ar[p], bk_rows[p], "nt") for p in pairs]
        yield
        low = [jnp.where(strict, x[:C, :LANES], 0.0) for x in mbk]
        ak = [jnp.where(strict, x[:C, LANES:], 0.0) for x in mbk]
        rbk = [jnp.where(lower2, x[C:], 0.0) for x in mbk]

        quads = range(N_PAIRS // 2)
        low4 = [_cat([low[2 * i], low[2 * i + 1]], 1) for i in quads]
        ld = [jnp.where(same_block[8], x, 0.0) for x in low4]
        l2 = prod_bf16(ld, ld)
        yield
        l4 = prod_bf16(l2, l2)
        l3 = prod_bf16(ld, l2)
        yield
        p1 = [eye + ld[i] + l2[i] + l3[i] for i in quads]
        p1l4 = prod_bf16(p1, l4)
        yield
        t4 = [p1[i] + p1l4[i] for i in quads]
        b = 8
        while b < C:
            couple = jnp.logical_not(same_block[b])
            if 2 * b < C:
                couple = jnp.logical_and(same_block[2 * b], couple)
            x = [jnp.where(couple, y, 0.0) for y in low4]
            tx = prod_bf16(t4, x)
            yield
            txt = prod_bf16(tx, t4)
            yield
            t4 = [t4[i] + txt[i] for i in quads]
            b *= 2
        t = [t4[p // 2][:, (p % 2) * LANES:(p % 2 + 1) * LANES] for p in pairs]
        akv = prod(ak, v)
        yield
        scale = [x[C - 1:C, :] for x in decay_in]
        out.update(
            v=v, ar=ar, rsp=[_split2(x) for x in rbk], t=t, akv=akv,
            bk=[_split2(_cat([b_t[p] * scale[p], k_t[p] * scale[p]], 0)) for p in pairs],
            decay_cols=[x[C - 8:C, :].T[:, 7:8] for x in decay_in])

    def state_step(c, pre, z):
        rows = slice(c * C, (c + 1) * C)
        st = [_mm_left(*pre["ar"][p], z[p].astype(BF16), "nn") for p in pairs]
        yield
        rhs = [st[p][:C] + pre["akv"][p] for p in pairs]
        u = prod(pre["t"], rhs)
        yield
        ub = [x.astype(BF16) for x in u]
        vb = [x.astype(BF16) for x in pre["v"]]
        uv = [_cat([blockdiag(ub[p]), blockdiag(vb[p])], 0) for p in pairs]
        y = [st[p][C:] + _mm_left(*pre["rsp"][p], uv[p], "nn") for p in pairs]
        upd = [_mm_left(*pre["bk"][p], _cat([ub[p], vb[p]], 0), "tn") for p in pairs]
        yield
        for p in pairs:
            z[p] = pre["decay_cols"][p] * z[p] + jnp.where(head_bd, upd[p], 0.0)
        mu = [_mm2(x, avg_bd2) for x in y]
        yield
        d = [y[p] - mu[p] for p in pairs]
        var = [_mm2(x * x, avg_bd2) for x in d]
        yield
        for p in pairs:
            yn = d[p] * lax.rsqrt(var[p] + LNX_EPS) * lng_ref[p] + lnb_ref[p]
            o_ref[0, p, rows, :] = (yn + bonus_ref[0, p, rows, :]).astype(o_ref.dtype)

    z = [z_ref[p] for p in pairs]
    prepared, running = {}, {}

    def start_more():
        while len(running) < SCAN_CHUNKS_AHEAD and len(prepared) < chunks_per_step:
            c_new = len(prepared)
            prepared[c_new] = {}
            running[c_new] = state_free(c_new, prepared[c_new])

    def advance():
        for c_run in list(running):
            if next(running[c_run], "done") == "done":
                del running[c_run]
        start_more()

    start_more()
    for c in range(chunks_per_step):
        while c in running:
            advance()
        for _ in state_step(c, prepared[c], z):
            advance()
    for p in pairs:
        z_ref[p] = z[p]


def _rwkv_scan(r, k, v, lw, kk, bb, bonus, lnx_g, lnx_b, chunks_per_step):
    batch, _, seq, _ = r.shape
    tc = CHUNK * chunks_per_step
    blk = pl.BlockSpec((1, N_PAIRS, tc, LANES), lambda b, c: (b, 0, c, 0))
    vec = _const_spec((N_PAIRS, 1, LANES))
    return pl.pallas_call(
        functools.partial(_rwkv_scan_kernel, chunks_per_step=chunks_per_step),
        grid=(batch, seq // tc),
        in_specs=[blk] * 7 + [vec, vec],
        out_specs=blk,
        out_shape=jax.ShapeDtypeStruct(r.shape, BF16),
        scratch_shapes=[pltpu.VMEM((N_PAIRS, LANES, LANES), F32)],
        compiler_params=_params(2),
        name="rwkv_scan",
    )(r, k, v, lw, kk, bb, bonus, lnx_g, lnx_b)


def kernel(x, moba_norm_g, moba_w_in, moba_w_out, rwkv_norm_g, rwkv_mix, rwkv_w_in, rwkv_w0, rwkv_w1, rwkv_w2, rwkv_a0, rwkv_a1, rwkv_a2, rwkv_k_k, rwkv_k_a, rwkv_r_k, rwkv_lnx_g, rwkv_lnx_b, rwkv_w_out, final_norm_g):
    batch, seq, d = x.shape
    assert d == D_MODEL and seq % MOBA_BLOCK == 0 and seq // MOBA_BLOCK <= 32
    x2d = x.reshape(batch * seq, d)
    row = lambda t: t.reshape(1, -1).astype(F32)

    q, k, v, sg = _moba_proj(x2d, row(moba_norm_g), moba_w_in.astype(BF16), batch, seq, tm=512)
    slopes = jnp.asarray([2.0 ** (-8.0 * (i + 1) / N_HEADS) for i in range(N_HEADS)], F32)
    slope_hi = (slopes * LOG2E).astype(BF16).astype(F32)
    sfeat = jnp.stack([slope_hi, slopes * LOG2E - slope_hi], axis=1).reshape(-1)
    o = _moba_attn(sfeat, slopes, q, k, v, unroll=2, tiles=4)
    x1 = _out_proj(o, sg, x2d, moba_w_out.astype(BF16), row(final_norm_g), seq, tm=512,
                   final_norm=False)

    weights = (row(rwkv_norm_g), rwkv_mix.astype(F32), rwkv_w_in.astype(BF16),
               row(rwkv_w0), rwkv_w1.T.astype(BF16), rwkv_w2.astype(BF16),
               row(rwkv_a0), rwkv_a1.T.astype(BF16), rwkv_a2.astype(BF16),
               row(rwkv_k_k), row(rwkv_k_a), row(rwkv_r_k))
    r, k2, v2, lw, kk, bb, bonus, sg2 = _rwkv_proj(x1, weights, batch, seq, tm=512)
    pair_vec = lambda t: t.reshape(N_PAIRS, 1, LANES).astype(F32)
    o2 = _rwkv_scan(r, k2, v2, lw, kk, bb, bonus, pair_vec(rwkv_lnx_g), pair_vec(rwkv_lnx_b),
                    chunks_per_step=8)
    out = _out_proj(o2, sg2, x1, rwkv_w_out.astype(BF16), row(final_norm_g), seq, tm=512,
                    final_norm=True)
    return out.reshape(batch, seq, d)
```

```python
import functools
import math

import jax
import jax.numpy as jnp
from jax import lax
from jax.experimental import pallas as pl
from jax.experimental.pallas import tpu as pltpu

F32 = jnp.float32
BF16 = jnp.bfloat16

D_MODEL = 1024
N_HEADS = 16
HEAD_DIM = 64
WIDTH = N_HEADS * HEAD_DIM
LANES = 128
N_PAIRS = WIDTH // LANES
MOBA_BLOCK = 256
MOBA_TOPK = 3
BF16_SUBLANES = 16
V_ROWS = HEAD_DIM + BF16_SUBLANES
LORA = 64
NORM_EPS = 1e-6
LNX_EPS = 64e-5
CHUNK = 64
SCAN_CHUNKS_AHEAD = 3
SCAN_CHUNKS_PER_STEP = 8
PROJ_ROWS = 512
ATTN_TILES = 4
ATTN_BLOCKS_PER_GROUP = 2
VMEM_LIMIT = 56 * 1024 * 1024
NEG_INF = float("-inf")
POS_INF = float("inf")
LOG2E = math.log2(math.e)


def _dot(a, b):
    return jnp.dot(a, b, preferred_element_type=F32)


def _dot_nt(a, b):
    return lax.dot_general(a, b, (((1,), (1,)), ((), ())), preferred_element_type=F32)


def _dot_tn(a, b):
    return lax.dot_general(a, b, (((0,), (0,)), ((), ())), preferred_element_type=F32)


def _split2(x):
    hi = x.astype(BF16)
    lo = (x - hi.astype(F32)).astype(BF16)
    return hi, lo


def _split3(x):
    hi = x.astype(BF16)
    r1 = x - hi.astype(F32)
    mid = r1.astype(BF16)
    lo = (r1 - mid.astype(F32)).astype(BF16)
    return hi, mid, lo


def _cat(xs, axis):
    return jnp.concatenate(xs, axis=axis)


def _mm3_parts(ah, al, bh, bl, kind):
    if kind == "nn":
        return _dot(_cat([ah, al], 1), _cat([bh, bh], 0)) + _dot(ah, bl)
    if kind == "nt":
        return _dot_nt(_cat([ah, al], 1), _cat([bh, bh], 1)) + _dot_nt(ah, bl)
    assert kind == "tn"
    return _dot_tn(_cat([ah, al], 0), _cat([bh, bh], 0)) + _dot_tn(ah, bl)


def _mm_left(ah, al, b, kind):
    if kind == "nn":
        return _dot(_cat([ah, al], 1), _cat([b, b], 0))
    if kind == "nt":
        return _dot_nt(_cat([ah, al], 1), _cat([b, b], 1))
    assert kind == "tn"
    return _dot_tn(_cat([ah, al], 0), _cat([b, b], 0))


def _mm3(a, b, kind="nn"):
    ah, al = _split2(a)
    bh, bl = _split2(b)
    return _mm3_parts(ah, al, bh, bl, kind)


def _mm2(a, b2_exact):
    ah, al = _split2(a)
    return _dot(_cat([ah, al], 1), b2_exact)


def _rms(x, g):
    ms = jnp.mean(x * x, axis=-1, keepdims=True)
    return x * lax.rsqrt(ms + NORM_EPS) * g


def _iota(shape, dim):
    return lax.broadcasted_iota(jnp.int32, shape, dim)


def _head_blockdiag(n):
    return (_iota((n, n), 0) // HEAD_DIM) == (_iota((n, n), 1) // HEAD_DIM)


def _const_spec(shape):
    nd = len(shape)
    return pl.BlockSpec(shape, lambda *_: (0,) * nd)


def _params(n_axes):
    return pltpu.CompilerParams(dimension_semantics=("arbitrary",) * n_axes,
                                vmem_limit_bytes=VMEM_LIMIT)


def _moba_proj_kernel(x_ref, g_ref, w_ref, q_ref, k_ref, v_ref, sg_ref):
    h = _rms(x_ref[...], g_ref[...]).astype(BF16)
    q = _dot(h, w_ref[:, 0 * WIDTH:1 * WIDTH])
    k = _dot(h, w_ref[:, 1 * WIDTH:2 * WIDTH])
    v = _dot(h, w_ref[:, 2 * WIDTH:3 * WIDTH])
    gate = _dot(h, w_ref[:, 3 * WIDTH:4 * WIDTH])
    for p in range(N_PAIRS):
        cols = slice(p * LANES, (p + 1) * LANES)
        q_ref[0, p] = q[:, cols]
        k_ref[0, p] = k[:, cols].astype(BF16)
        v_ref[0, p] = v[:, cols].astype(BF16)
    sg_ref[...] = (gate * jax.nn.sigmoid(gate)).astype(BF16)


def _moba_proj(x2d, norm_g, w_bf16, batch, seq, tm):
    m = x2d.shape[0]
    nsb = seq // tm
    pair_spec = pl.BlockSpec((1, N_PAIRS, tm, LANES), lambda i: (i // nsb, 0, i % nsb, 0))
    pair_shape = (batch, N_PAIRS, seq, LANES)
    return pl.pallas_call(
        _moba_proj_kernel,
        grid=(m // tm,),
        in_specs=[
            pl.BlockSpec((tm, D_MODEL), lambda i: (i, 0)),
            _const_spec((1, D_MODEL)),
            _const_spec((D_MODEL, 4 * WIDTH)),
        ],
        out_specs=[pair_spec, pair_spec, pair_spec,
                   pl.BlockSpec((tm, WIDTH), lambda i: (i, 0))],
        out_shape=[
            jax.ShapeDtypeStruct(pair_shape, F32),
            jax.ShapeDtypeStruct(pair_shape, BF16),
            jax.ShapeDtypeStruct(pair_shape, BF16),
            jax.ShapeDtypeStruct((m, WIDTH), BF16),
        ],
        compiler_params=_params(1),
        name="moba_proj",
    )(x2d, norm_g, w_bf16)


def _moba_attn_kernel(sfeat_ref, slopes_ref, q_ref, k_ref, v_ref, o_ref,
                      kmean_ref, kaug_ref, vt_ref, s_ref, acc_ref, *, nb, unroll, tiles):
    L = MOBA_BLOCK
    p = pl.program_id(1)
    n = pl.program_id(2)
    lane = _iota((L, LANES), 1)
    head_lo = lane < HEAD_DIM
    head_masks = (head_lo, jnp.logical_not(head_lo))
    feat_base = (HEAD_DIM, 0)

    @pl.when(n == 0)
    def _():
        offs = _iota((L, LANES), 0).astype(F32)
        ones_row = (_iota((V_ROWS - HEAD_DIM, L), 0) == 0).astype(BF16)
        for j in range(nb):
            rows = slice(j * L, (j + 1) * L)
            kf = k_ref[0, 0, rows, :].astype(F32)
            kmean_ref[j:j + 1, :] = jnp.mean(kf, axis=0, keepdims=True)
            for hh in range(2):
                fb = feat_base[hh]
                feat = jnp.where(jnp.logical_or(lane == fb, lane == fb + 1), offs, 0.0)
                kaug_ref[hh, rows, :] = jnp.where(head_masks[hh], kf, feat).astype(BF16)
            v_t = v_ref[0, 0, rows, :].astype(F32).T.astype(BF16)
            for hh in range(2):
                vt_ref[j, hh, :HEAD_DIM, :] = v_t[hh * HEAD_DIM:(hh + 1) * HEAD_DIM, :]
                vt_ref[j, hh, HEAD_DIM:, :] = ones_row

    n_first = n * tiles
    units = [(t, hh) for t in range(tiles) for hh in range(2)]
    blk = _iota((nb, L), 0)
    kmean = kmean_ref[...]
    key_i = _iota((L, L), 0)
    qry_i = _iota((L, L), 1)

    qa, bits = [], []
    for t, hh in units:
        h = 2 * p + hh
        qf = q_ref[0, 0, t * L:(t + 1) * L, :]
        gate = _mm3(kmean, jnp.where(head_masks[hh], qf, 0.0), "nt")
        gate = jnp.where(blk < n_first + t, gate, NEG_INF)
        chosen_bits = jnp.zeros((1, L), jnp.int32)
        for _ in range(MOBA_TOPK):
            mx = jnp.max(gate, axis=0, keepdims=True)
            first = jnp.min(jnp.where(gate == mx, blk, nb), axis=0, keepdims=True)
            valid = mx > NEG_INF
            chosen_bits = chosen_bits | jnp.where(valid, jnp.left_shift(1, first), 0)
            gate = jnp.where(jnp.logical_and(blk == first, valid), NEG_INF, gate)
        bits.append(chosen_bits)
        fb = feat_base[hh]
        feat = jnp.where(lane == fb, sfeat_ref[2 * h],
                         jnp.where(lane == fb + 1, sfeat_ref[2 * h + 1], 0.0))
        qa.append(jnp.where(head_masks[hh], qf * (HEAD_DIM ** -0.5 * LOG2E), feat).astype(BF16))
    qa_all = [_cat([qa[2 * t + hh] for t in range(tiles)], 0) for hh in range(2)]

    nu = len(units)

    def issue_scores(g, slot):
        off = pl.multiple_of(jnp.minimum(g * unroll, nb - unroll) * L, L)
        per_head = [_dot_nt(kaug_ref[hh, pl.ds(off, unroll * L), :], qa_all[hh])
                    for hh in range(2)]
        maxes = []
        for u in range(unroll):
            for w, (t, hh) in enumerate(units):
                s = per_head[hh][u * L:(u + 1) * L, t * L:(t + 1) * L]
                s_ref[slot, nu * u + w] = s
                maxes.append(jnp.max(s, axis=0, keepdims=True))
        return maxes

    def consume(g, slot, maxes, ms):
        ms = list(ms)
        for u in range(unroll):
            j = g * unroll + u
            jc = jnp.minimum(j, nb - 1)
            for w, (t, hh) in enumerate(units):
                tile = n_first + t
                c = ((j - tile) * L).astype(F32) * LOG2E * slopes_ref[2 * p + hh]
                chosen = jnp.logical_and((jnp.right_shift(bits[w], jc) & 1) == 1, j < tile)
                m_new = jnp.maximum(ms[w], jnp.where(chosen, maxes[nu * u + w] + c, NEG_INF))
                alpha = jnp.exp2(ms[w] - m_new)
                shift = jnp.where(chosen, m_new - c, POS_INF)
                pr = jnp.exp2(s_ref[slot, nu * u + w] - shift).astype(BF16)
                acc_ref[w] = alpha * acc_ref[w] + _dot(vt_ref[jc, hh], pr)
                ms[w] = m_new
        return ms

    own = [_dot_nt(kaug_ref[hh, pl.ds(pl.multiple_of((n_first + t) * L, L), L), :], qa[2 * t + hh])
           for t, hh in units]
    first_maxes = issue_scores(0, 0)
    ms = []
    for w, (t, hh) in enumerate(units):
        tile = n_first + t
        s = jnp.where(key_i <= qry_i, own[w], NEG_INF)
        m = jnp.max(s, axis=0, keepdims=True)
        pr = jnp.exp2(s - m).astype(BF16)
        acc_ref[w] = _dot(vt_ref[tile, hh], pr)
        ms.append(m)

    nm = nu * unroll

    def body(i, carry):
        maxes, ms = carry[:nm], carry[nm:]
        maxes_b = issue_scores(2 * i + 1, 1)
        ms = consume(2 * i, 0, maxes, ms)
        maxes_a = issue_scores(2 * i + 2, 0)
        ms = consume(2 * i + 1, 1, maxes_b, ms)
        return (*maxes_a, *ms)

    per_trip = 2 * unroll
    n_past = n_first + tiles - 1
    lax.fori_loop(0, (n_past + per_trip - 1) // per_trip, body, (*first_maxes, *ms))
    for t in range(tiles):
        acc0, acc1 = acc_ref[2 * t], acc_ref[2 * t + 1]
        o_t = _cat([acc0[:HEAD_DIM] / acc0[HEAD_DIM:HEAD_DIM + 1],
                    acc1[:HEAD_DIM] / acc1[HEAD_DIM:HEAD_DIM + 1]], 0)
        o_ref[0, 0, t * L:(t + 1) * L, :] = o_t.T.astype(o_ref.dtype)


def _moba_attn(sfeat, slopes, q, k, v, unroll, tiles):
    batch, _, seq, _ = q.shape
    nb = seq // MOBA_BLOCK
    assert nb % tiles == 0
    tile = pl.BlockSpec((1, 1, tiles * MOBA_BLOCK, LANES), lambda b, p, n: (b, p, n, 0))
    full = pl.BlockSpec((1, 1, seq, LANES), lambda b, p, n: (b, p, 0, 0))
    smem = pl.BlockSpec(memory_space=pltpu.SMEM)
    return pl.pallas_call(
        functools.partial(_moba_attn_kernel, nb=nb, unroll=unroll, tiles=tiles),
        grid=(batch, N_PAIRS, nb // tiles),
        in_specs=[smem, smem, tile, full, full],
        out_specs=tile,
        out_shape=jax.ShapeDtypeStruct(q.shape, BF16),
        scratch_shapes=[pltpu.VMEM((nb, LANES), F32),
                        pltpu.VMEM((2, seq, LANES), BF16),
                        pltpu.VMEM((nb, 2, V_ROWS, MOBA_BLOCK), BF16),
                        pltpu.VMEM((2, 2 * tiles * unroll, MOBA_BLOCK, MOBA_BLOCK), F32),
                        pltpu.VMEM((2 * tiles, V_ROWS, MOBA_BLOCK), F32)],
        compiler_params=_params(3),
        name="moba_attn",
    )(sfeat, slopes, q, k, v)


def _out_proj_kernel(o_ref, sg_ref, x_ref, w_ref, g_ref, y_ref, *, final_norm):
    o = _cat([o_ref[0, p] for p in range(N_PAIRS)], 1)
    gated = (o.astype(F32) * sg_ref[...].astype(F32)).astype(BF16)
    y = x_ref[...] + _dot(gated, w_ref[...])
    if final_norm:
        y = _rms(y, g_ref[...])
    y_ref[...] = y


def _out_proj(o_pairs, sg, x2d, w_bf16, norm_g, seq, tm, final_norm):
    m = x2d.shape[0]
    nsb = seq // tm
    row_spec = pl.BlockSpec((tm, D_MODEL), lambda i: (i, 0))
    return pl.pallas_call(
        functools.partial(_out_proj_kernel, final_norm=final_norm),
        grid=(m // tm,),
        in_specs=[
            pl.BlockSpec((1, N_PAIRS, tm, LANES), lambda i: (i // nsb, 0, i % nsb, 0)),
            row_spec, row_spec,
            _const_spec((WIDTH, D_MODEL)),
            _const_spec((1, D_MODEL)),
        ],
        out_specs=row_spec,
        out_shape=jax.ShapeDtypeStruct((m, D_MODEL), F32),
        compiler_params=_params(1),
        name="out_proj_final" if final_norm else "out_proj",
    )(o_pairs, sg, x2d, w_bf16, norm_g)


def _rwkv_proj_kernel(x_ref, halo_ref, g_ref, mix_ref, win_ref, w0_ref, w1t_ref, w2_ref,
                      a0_ref, a1t_ref, a2_ref, kk_ref, ka_ref, rk_ref,
                      r_out, k_out, v_out, lw_out, kk_out, bb_out, bonus_out, sg_out,
                      *, tiles_per_seq):
    i = pl.program_id(0)
    tm = x_ref.shape[0]
    g = g_ref[...]
    h = _rms(x_ref[...], g)
    prev_row = _rms(halo_ref[...], g)[7:8, :]
    prev_row = jnp.where(i % tiles_per_seq == 0, 0.0, prev_row)
    rolled = pltpu.roll(h, 1, 0)
    h_prev = jnp.where(_iota((tm, D_MODEL), 0) == 0, prev_row, rolled)
    xx = h_prev - h

    def stream(n):
        return h + xx * mix_ref[n:n + 1, :]

    r = _dot(stream(0).astype(BF16), win_ref[0])
    k = _dot(stream(1).astype(BF16), win_ref[1])
    v = _dot(stream(2).astype(BF16), win_ref[2])
    gt = _dot(stream(3).astype(BF16), win_ref[3])
    sg_out[...] = (gt * jax.nn.sigmoid(gt)).astype(BF16)

    def lora(xs, down_t_ref, up_ref, act):
        mid_t = act(_dot_nt(down_t_ref[...], xs.astype(BF16)))
        return _dot_tn(mid_t.astype(BF16), up_ref[...])

    z = -(w0_ref[...] + lora(stream(4), w1t_ref, w2_ref, jnp.tanh))
    softplus = jnp.maximum(z, 0.0) + jnp.log(1.0 + jnp.exp(-jnp.abs(z)))
    lw = -jnp.exp(-softplus - 0.5)
    a = jax.nn.sigmoid(a0_ref[...] + lora(stream(5), a1t_ref, a2_ref, lambda t: t))
    kr = k * kk_ref[...]
    k_mod = k * (1.0 + (a - 1.0) * ka_ref[...])
    rk = r * k_mod * rk_ref[...]
    ones_bd = _head_blockdiag(LANES).astype(BF16)
    for p in range(N_PAIRS):
        cols = slice(p * LANES, (p + 1) * LANES)
        kr_p = kr[:, cols]
        ss = _dot((kr_p * kr_p).astype(BF16), ones_bd)
        kk_p = kr_p * jnp.minimum(lax.rsqrt(ss), 1e12)
        r_out[0, p] = r[:, cols]
        k_out[0, p] = k_mod[:, cols]
        v_out[0, p] = v[:, cols]
        lw_out[0, p] = lw[:, cols]
        kk_out[0, p] = kk_p
        bb_out[0, p] = kk_p * a[:, cols]
        bonus_out[0, p] = _dot(rk[:, cols].astype(BF16), ones_bd) * v[:, cols]


def _rwkv_proj(x2d, weights, batch, seq, tm):
    m = x2d.shape[0]
    nsb = seq // tm
    halo_blocks = tm // 8
    pair_spec = pl.BlockSpec((1, N_PAIRS, tm, LANES), lambda i: (i // nsb, 0, i % nsb, 0))
    pair_shape = jax.ShapeDtypeStruct((batch, N_PAIRS, seq, LANES), F32)
    vec = _const_spec((1, WIDTH))
    lora_w = _const_spec((LORA, WIDTH))
    return pl.pallas_call(
        functools.partial(_rwkv_proj_kernel, tiles_per_seq=nsb),
        grid=(m // tm,),
        in_specs=[
            pl.BlockSpec((tm, D_MODEL), lambda i: (i, 0)),
            pl.BlockSpec((8, D_MODEL), lambda i: (jnp.maximum(i * halo_blocks - 1, 0), 0)),
            _const_spec((1, D_MODEL)),
            _const_spec((6, D_MODEL)),
            _const_spec((4, D_MODEL, WIDTH)),
            vec, lora_w, lora_w,
            vec, lora_w, lora_w,
            vec, vec, vec,
        ],
        out_specs=[pair_spec] * 7 + [pl.BlockSpec((tm, WIDTH), lambda i: (i, 0))],
        out_shape=[pair_shape] * 7 + [jax.ShapeDtypeStruct((m, WIDTH), BF16)],
        compiler_params=_params(1),
        name="rwkv_proj",
    )(x2d, x2d, *weights)


def _rwkv_scan_kernel(r_ref, k_ref, v_ref, lw_ref, kk_ref, bb_ref, bonus_ref, lng_ref, lnb_ref,
                      o_ref, z_ref, *, chunks_per_step):
    C = CHUNK
    pairs = range(N_PAIRS)

    @pl.when(pl.program_id(1) == 0)
    def _():
        z_ref[...] = jnp.zeros_like(z_ref)

    row = _iota((C, LANES), 0)
    lane = _iota((C, LANES), 1)
    col = lane % C
    head_lo = lane < HEAD_DIM
    keep_lo = head_lo.astype(BF16)
    keep_hi = jnp.logical_not(head_lo).astype(BF16)
    strict = col < row
    row4 = _iota((C, 2 * LANES), 0)
    lane4 = _iota((C, 2 * LANES), 1)
    col4 = lane4 % C
    lower2 = col4 <= row4
    eye = (row4 == col4).astype(F32)
    same_block = {b: (row4 // b) == (col4 // b) for b in (8, 16, 32)}
    keep_head = [(lane4 // HEAD_DIM == h).astype(BF16) for h in range(4)]
    tri3 = ((_iota((C, 3 * C), 1) % C) <= _iota((C, 3 * C), 0)).astype(BF16)
    head_bd = _head_blockdiag(LANES)
    avg_bd = head_bd.astype(BF16) * (1.0 / HEAD_DIM)
    avg_bd2 = _cat([avg_bd, avg_bd], 0)

    def blockdiag(xb):
        return _cat([xb * keep_lo, xb * keep_hi], 0)

    def prod(a_list, b_list):
        parts = [(_split2(a), b.astype(BF16)) for a, b in zip(a_list, b_list)]
        return [_mm_left(ah, al, blockdiag(bb), "nn") for (ah, al), bb in parts]

    def prod_bf16(a_list, b_list):
        parts = [(a.astype(BF16), b.astype(BF16)) for a, b in zip(a_list, b_list)]
        return [_dot(ab, _cat([bb * keep for keep in keep_head], 0)) for ab, bb in parts]

    def state_free(c, out):
        rows = slice(c * C, (c + 1) * C)
        v = [v_ref[0, p, rows, :] for p in pairs]
        lw = [lw_ref[0, p, rows, :] for p in pairs]
        lw3 = [_cat(_split3(x), 0) for x in lw]
        cs = [_dot(tri3, x) for x in lw3]
        yield
        decay_in = [jnp.exp(x) for x in cs]
        inv = [jnp.exp(-x) for x in cs]
        r_t = [r_ref[0, p, rows, :] * decay_in[p] for p in pairs]
        k_t = [k_ref[0, p, rows, :] * inv[p] for p in pairs]
        b_t = [bb_ref[0, p, rows, :] * inv[p] for p in pairs]
        a_t = [-(kk_ref[0, p, rows, :] * jnp.exp(cs[p] - lw[p])) for p in pairs]
        ar = [_split2(_cat([a_t[p], r_t[p]], 0)) for p in pairs]
        bk_rows = [_cat([blockdiag(b_t[p].astype(BF16)), blockdiag(k_t[p].astype(BF16))], 0)
                   for p in pairs]
        mbk = [_mm_left(*ar[p], bk_rows[p], "nt") for p in pairs]
        yield
        low = [jnp.where(strict, x[:C, :LANES], 0.0) for x in mbk]
        ak = [jnp.where(strict, x[:C, LANES:], 0.0) for x in mbk]
        rbk = [jnp.where(lower2, x[C:], 0.0) for x in mbk]

        quads = range(N_PAIRS // 2)
        low4 = [_cat([low[2 * i], low[2 * i + 1]], 1) for i in quads]
        ld = [jnp.where(same_block[8], x, 0.0) for x in low4]
        l2 = prod_bf16(ld, ld)
        yield
        l4 = prod_bf16(l2, l2)
        l3 = prod_bf16(ld, l2)
        yield
        p1 = [eye + ld[i] + l2[i] + l3[i] for i in quads]
        p1l4 = prod_bf16(p1, l4)
        yield
        t4 = [p1[i] + p1l4[i] for i in quads]
        b = 8
        while b < C:
            couple = jnp.logical_not(same_block[b])
            if 2 * b < C:
                couple = jnp.logical_and(same_block[2 * b], couple)
            x = [jnp.where(couple, y, 0.0) for y in low4]
            tx = prod_bf16(t4, x)
            yield
            txt = prod_bf16(tx, t4)
            yield
            t4 = [t4[i] + txt[i] for i in quads]
            b *= 2
        t = [t4[p // 2][:, (p % 2) * LANES:(p % 2 + 1) * LANES] for p in pairs]
        akv = prod(ak, v)
        yield
        scale = [x[C - 1:C, :] for x in decay_in]
        out.update(
            v=v, ar=ar, rsp=[_split2(x) for x in rbk], t=t, akv=akv,
            bk=[_split2(_cat([b_t[p] * scale[p], k_t[p] * scale[p]], 0)) for p in pairs],
            decay_cols=[x[C - 8:C, :].T[:, 7:8] for x in decay_in])

    def state_step(c, pre, z):
        rows = slice(c * C, (c + 1) * C)
        st = [_mm_left(*pre["ar"][p], z[p].astype(BF16), "nn") for p in pairs]
        yield
        rhs = [st[p][:C] + pre["akv"][p] for p in pairs]
        u = prod(pre["t"], rhs)
        yield
        ub = [x.astype(BF16) for x in u]
        vb = [x.astype(BF16) for x in pre["v"]]
        uv = [_cat([blockdiag(ub[p]), blockdiag(vb[p])], 0) for p in pairs]
        y = [st[p][C:] + _mm_left(*pre["rsp"][p], uv[p], "nn") for p in pairs]
        upd = [_mm_left(*pre["bk"][p], _cat([ub[p], vb[p]], 0), "tn") for p in pairs]
        yield
        for p in pairs:
            z[p] = pre["decay_cols"][p] * z[p] + jnp.where(head_bd, upd[p], 0.0)
        mu = [_mm2(x, avg_bd2) for x in y]
        yield
        d = [y[p] - mu[p] for p in pairs]
        var = [_mm2(x * x, avg_bd2) for x in d]
        yield
        for p in pairs:
            yn = d[p] * lax.rsqrt(var[p] + LNX_EPS) * lng_ref[p] + lnb_ref[p]
            o_ref[0, p, rows, :] = (yn + bonus_ref[0, p, rows, :]).astype(o_ref.dtype)

    z = [z_ref[p] for p in pairs]
    prepared, running = {}, {}

    def start_more():
        while len(running) < SCAN_CHUNKS_AHEAD and len(prepared) < chunks_per_step:
            c_new = len(prepared)
            prepared[c_new] = {}
            running[c_new] = state_free(c_new, prepared[c_new])

    def advance():
        for c_run in list(running):
            if next(running[c_run], "done") == "done":
                del running[c_run]
        start_more()

    start_more()
    for c in range(chunks_per_step):
        while c in running:
            advance()
        for _ in state_step(c, prepared[c], z):
            advance()
    for p in pairs:
        z_ref[p] = z[p]


def _rwkv_scan(r, k, v, lw, kk, bb, bonus, lnx_g, lnx_b, chunks_per_step):
    batch, _, seq, _ = r.shape
    tc = CHUNK * chunks_per_step
    blk = pl.BlockSpec((1, N_PAIRS, tc, LANES), lambda b, c: (b, 0, c, 0))
    vec = _const_spec((N_PAIRS, 1, LANES))
    return pl.pallas_call(
        functools.partial(_rwkv_scan_kernel, chunks_per_step=chunks_per_step),
        grid=(batch, seq // tc),
        in_specs=[blk] * 7 + [vec, vec],
        out_specs=blk,
        out_shape=jax.ShapeDtypeStruct(r.shape, BF16),
        scratch_shapes=[pltpu.VMEM((N_PAIRS, LANES, LANES), F32)],
        compiler_params=_params(2),
        name="rwkv_scan",
    )(r, k, v, lw, kk, bb, bonus, lnx_g, lnx_b)


def kernel(x, moba_norm_g, moba_w_in, moba_w_out, rwkv_norm_g, rwkv_mix, rwkv_w_in, rwkv_w0, rwkv_w1, rwkv_w2, rwkv_a0, rwkv_a1, rwkv_a2, rwkv_k_k, rwkv_k_a, rwkv_r_k, rwkv_lnx_g, rwkv_lnx_b, rwkv_w_out, final_norm_g):
    batch, seq, d = x.shape
    nb = seq // MOBA_BLOCK
    assert d == D_MODEL and seq % MOBA_BLOCK == 0
    assert nb <= 32, "the chosen-block mask of a query is one int32"
    assert nb % ATTN_TILES == 0 and nb % ATTN_BLOCKS_PER_GROUP == 0
    assert seq % PROJ_ROWS == 0 and seq % (CHUNK * SCAN_CHUNKS_PER_STEP) == 0
    x2d = x.reshape(batch * seq, d)
    row = lambda t: t.reshape(1, -1).astype(F32)

    q, k, v, sg = _moba_proj(x2d, row(moba_norm_g), moba_w_in.astype(BF16), batch, seq,
                             tm=PROJ_ROWS)
    slopes = jnp.asarray([2.0 ** (-8.0 * (i + 1) / N_HEADS) for i in range(N_HEADS)], F32)
    slope_hi = (slopes * LOG2E).astype(BF16).astype(F32)
    sfeat = jnp.stack([slope_hi, slopes * LOG2E - slope_hi], axis=1).reshape(-1)
    o = _moba_attn(sfeat, slopes, q, k, v, unroll=ATTN_BLOCKS_PER_GROUP, tiles=ATTN_TILES)
    x1 = _out_proj(o, sg, x2d, moba_w_out.astype(BF16), row(final_norm_g), seq, tm=PROJ_ROWS,
                   final_norm=False)

    weights = (row(rwkv_norm_g), rwkv_mix.astype(F32), rwkv_w_in.astype(BF16),
               row(rwkv_w0), rwkv_w1.T.astype(BF16), rwkv_w2.astype(BF16),
               row(rwkv_a0), rwkv_a1.T.astype(BF16), rwkv_a2.astype(BF16),
               row(rwkv_k_k), row(rwkv_k_a), row(rwkv_r_k))
    r, k2, v2, lw, kk, bb, bonus, sg2 = _rwkv_proj(x1, weights, batch, seq, tm=PROJ_ROWS)
    pair_vec = lambda t: t.reshape(N_PAIRS, 1, LANES).astype(F32)
    o2 = _rwkv_scan(r, k2, v2, lw, kk, bb, bonus, pair_vec(rwkv_lnx_g), pair_vec(rwkv_lnx_b),
                    chunks_per_step=SCAN_CHUNKS_PER_STEP)
    out = _out_proj(o2, sg2, x1, rwkv_w_out.astype(BF16), row(final_norm_g), seq, tm=PROJ_ROWS,
                    final_norm=True)
    return out.reshape(batch, seq, d)
```

```python
import functools
import math

import jax
import jax.numpy as jnp
from jax import lax
from jax.experimental import pallas as pl
from jax.experimental.pallas import tpu as pltpu

F32 = jnp.float32
BF16 = jnp.bfloat16

D_MODEL = 1024
N_HEADS = 16
HEAD_DIM = 64
WIDTH = N_HEADS * HEAD_DIM
LANES = 128
N_PAIRS = WIDTH // LANES
MOBA_BLOCK = 256
MOBA_TOPK = 3
BF16_SUBLANES = 16
V_ROWS = HEAD_DIM + BF16_SUBLANES
LORA = 64
NORM_EPS = 1e-6
LNX_EPS = 64e-5
CHUNK = 64
SCAN_CHUNKS_AHEAD = 2
SCAN_CHUNKS_PER_STEP = 8
PROJ_ROWS = 512
ATTN_TILES = 4
ATTN_BLOCKS_PER_GROUP = 2
VMEM_LIMIT = 56 * 1024 * 1024
NEG_INF = float("-inf")
POS_INF = float("inf")
LOG2E = math.log2(math.e)


def _dot(a, b):
    return jnp.dot(a, b, preferred_element_type=F32)


def _dot_nt(a, b):
    return lax.dot_general(a, b, (((1,), (1,)), ((), ())), preferred_element_type=F32)


def _dot_tn(a, b):
    return lax.dot_general(a, b, (((0,), (0,)), ((), ())), preferred_element_type=F32)


def _split2(x):
    hi = x.astype(BF16)
    lo = (x - hi.astype(F32)).astype(BF16)
    return hi, lo


def _split3(x):
    hi = x.astype(BF16)
    r1 = x - hi.astype(F32)
    mid = r1.astype(BF16)
    lo = (r1 - mid.astype(F32)).astype(BF16)
    return hi, mid, lo


def _cat(xs, axis):
    return jnp.concatenate(xs, axis=axis)


def _mm3_parts(ah, al, bh, bl, kind):
    if kind == "nn":
        return _dot(_cat([ah, al], 1), _cat([bh, bh], 0)) + _dot(ah, bl)
    if kind == "nt":
        return _dot_nt(_cat([ah, al], 1), _cat([bh, bh], 1)) + _dot_nt(ah, bl)
    assert kind == "tn"
    return _dot_tn(_cat([ah, al], 0), _cat([bh, bh], 0)) + _dot_tn(ah, bl)


def _mm3(a, b, kind="nn"):
    ah, al = _split2(a)
    bh, bl = _split2(b)
    return _mm3_parts(ah, al, bh, bl, kind)


def _mm2(a, b2_exact):
    ah, al = _split2(a)
    return _dot(_cat([ah, al], 1), b2_exact)


def _rms(x, g):
    ms = jnp.mean(x * x, axis=-1, keepdims=True)
    return x * lax.rsqrt(ms + NORM_EPS) * g


def _iota(shape, dim):
    return lax.broadcasted_iota(jnp.int32, shape, dim)


def _head_blockdiag(n):
    return (_iota((n, n), 0) // HEAD_DIM) == (_iota((n, n), 1) // HEAD_DIM)


def _const_spec(shape):
    nd = len(shape)
    return pl.BlockSpec(shape, lambda *_: (0,) * nd)


def _params(n_axes):
    return pltpu.CompilerParams(dimension_semantics=("arbitrary",) * n_axes,
                                vmem_limit_bytes=VMEM_LIMIT)


def _moba_proj_kernel(x_ref, g_ref, w_ref, q_ref, k_ref, v_ref, sg_ref):
    h = _rms(x_ref[...], g_ref[...]).astype(BF16)
    q = _dot(h, w_ref[:, 0 * WIDTH:1 * WIDTH])
    k = _dot(h, w_ref[:, 1 * WIDTH:2 * WIDTH])
    v = _dot(h, w_ref[:, 2 * WIDTH:3 * WIDTH])
    gate = _dot(h, w_ref[:, 3 * WIDTH:4 * WIDTH])
    for p in range(N_PAIRS):
        cols = slice(p * LANES, (p + 1) * LANES)
        q_ref[0, p] = q[:, cols]
        k_ref[0, p] = k[:, cols].astype(BF16)
        v_ref[0, p] = v[:, cols].astype(BF16)
    sg_ref[...] = (gate * jax.nn.sigmoid(gate)).astype(BF16)


def _moba_proj(x2d, norm_g, w_bf16, batch, seq, tm):
    m = x2d.shape[0]
    nsb = seq // tm
    pair_spec = pl.BlockSpec((1, N_PAIRS, tm, LANES), lambda i: (i // nsb, 0, i % nsb, 0))
    pair_shape = (batch, N_PAIRS, seq, LANES)
    return pl.pallas_call(
        _moba_proj_kernel,
        grid=(m // tm,),
        in_specs=[
            pl.BlockSpec((tm, D_MODEL), lambda i: (i, 0)),
            _const_spec((1, D_MODEL)),
            _const_spec((D_MODEL, 4 * WIDTH)),
        ],
        out_specs=[pair_spec, pair_spec, pair_spec,
                   pl.BlockSpec((tm, WIDTH), lambda i: (i, 0))],
        out_shape=[
            jax.ShapeDtypeStruct(pair_shape, F32),
            jax.ShapeDtypeStruct(pair_shape, BF16),
            jax.ShapeDtypeStruct(pair_shape, BF16),
            jax.ShapeDtypeStruct((m, WIDTH), BF16),
        ],
        compiler_params=_params(1),
        name="moba_proj",
    )(x2d, norm_g, w_bf16)


def _moba_attn_kernel(sfeat_ref, slopes_ref, q_ref, k_ref, v_ref, o_ref,
                      kmean_ref, kaug_ref, vt_ref, s_ref, acc_ref, *, nb, unroll, tiles):
    L = MOBA_BLOCK
    p = pl.program_id(1)
    n = pl.program_id(2)
    lane = _iota((L, LANES), 1)
    head_lo = lane < HEAD_DIM
    head_masks = (head_lo, jnp.logical_not(head_lo))
    feat_base = (HEAD_DIM, 0)

    @pl.when(n == 0)
    def _():
        offs = _iota((L, LANES), 0).astype(F32)
        ones_row = (_iota((V_ROWS - HEAD_DIM, L), 0) == 0).astype(BF16)
        for j in range(nb):
            rows = slice(j * L, (j + 1) * L)
            kf = k_ref[0, 0, rows, :].astype(F32)
            kmean_ref[j:j + 1, :] = jnp.mean(kf, axis=0, keepdims=True)
            for hh in range(2):
                fb = feat_base[hh]
                feat = jnp.where(jnp.logical_or(lane == fb, lane == fb + 1), offs, 0.0)
                kaug_ref[hh, rows, :] = jnp.where(head_masks[hh], kf, feat).astype(BF16)
            v_t = v_ref[0, 0, rows, :].astype(F32).T.astype(BF16)
            for hh in range(2):
                vt_ref[j, hh, :HEAD_DIM, :] = v_t[hh * HEAD_DIM:(hh + 1) * HEAD_DIM, :]
                vt_ref[j, hh, HEAD_DIM:, :] = ones_row

    n_first = n * tiles
    units = [(t, hh) for t in range(tiles) for hh in range(2)]
    blk = _iota((nb, L), 0)
    kmean = kmean_ref[...]
    key_i = _iota((L, L), 0)
    qry_i = _iota((L, L), 1)

    qa, bits = [], []
    for t, hh in units:
        h = 2 * p + hh
        qf = q_ref[0, 0, t * L:(t + 1) * L, :]
        gate = _mm3(kmean, jnp.where(head_masks[hh], qf, 0.0), "nt")
        gate = jnp.where(blk < n_first + t, gate, NEG_INF)
        chosen_bits = jnp.zeros((1, L), jnp.int32)
        for _ in range(MOBA_TOPK):
            mx = jnp.max(gate, axis=0, keepdims=True)
            first = jnp.min(jnp.where(gate == mx, blk, nb), axis=0, keepdims=True)
            valid = mx > NEG_INF
            chosen_bits = chosen_bits | jnp.where(valid, jnp.left_shift(1, first), 0)
            gate = jnp.where(jnp.logical_and(blk == first, valid), NEG_INF, gate)
        bits.append(chosen_bits)
        fb = feat_base[hh]
        feat = jnp.where(lane == fb, sfeat_ref[2 * h],
                         jnp.where(lane == fb + 1, sfeat_ref[2 * h + 1], 0.0))
        qa.append(jnp.where(head_masks[hh], qf * (HEAD_DIM ** -0.5 * LOG2E), feat).astype(BF16))
    qa_all = [_cat([qa[2 * t + hh] for t in range(tiles)], 0) for hh in range(2)]

    nu = len(units)

    def issue_scores(g, slot):
        off = pl.multiple_of(jnp.minimum(g * unroll, nb - unroll) * L, L)
        per_head = [_dot_nt(kaug_ref[hh, pl.ds(off, unroll * L), :], qa_all[hh])
                    for hh in range(2)]
        maxes = []
        for u in range(unroll):
            for w, (t, hh) in enumerate(units):
                s = per_head[hh][u * L:(u + 1) * L, t * L:(t + 1) * L]
                s_ref[slot, nu * u + w] = s
                maxes.append(jnp.max(s, axis=0, keepdims=True))
        return maxes

    def consume(g, slot, maxes, ms):
        ms = list(ms)
        for u in range(unroll):
            j = g * unroll + u
            jc = jnp.minimum(j, nb - 1)
            for w, (t, hh) in enumerate(units):
                tile = n_first + t
                c = ((j - tile) * L).astype(F32) * LOG2E * slopes_ref[2 * p + hh]
                chosen = jnp.logical_and((jnp.right_shift(bits[w], jc) & 1) == 1, j < tile)
                m_new = jnp.maximum(ms[w], jnp.where(chosen, maxes[nu * u + w] + c, NEG_INF))
                alpha = jnp.exp2(ms[w] - m_new)
                shift = jnp.where(chosen, m_new - c, POS_INF)
                pr = jnp.exp2(s_ref[slot, nu * u + w] - shift).astype(BF16)
                acc_ref[w] = alpha * acc_ref[w] + _dot(vt_ref[jc, hh], pr)
                ms[w] = m_new
        return ms

    own = [_dot_nt(kaug_ref[hh, pl.ds(pl.multiple_of((n_first + t) * L, L), L), :], qa[2 * t + hh])
           for t, hh in units]
    first_maxes = issue_scores(0, 0)
    ms = []
    for w, (t, hh) in enumerate(units):
        tile = n_first + t
        s = jnp.where(key_i <= qry_i, own[w], NEG_INF)
        m = jnp.max(s, axis=0, keepdims=True)
        pr = jnp.exp2(s - m).astype(BF16)
        acc_ref[w] = _dot(vt_ref[tile, hh], pr)
        ms.append(m)

    nm = nu * unroll

    def body(i, carry):
        maxes, ms = carry[:nm], carry[nm:]
        maxes_b = issue_scores(2 * i + 1, 1)
        ms = consume(2 * i, 0, maxes, ms)
        maxes_a = issue_scores(2 * i + 2, 0)
        ms = consume(2 * i + 1, 1, maxes_b, ms)
        return (*maxes_a, *ms)

    per_trip = 2 * unroll
    n_past = n_first + tiles - 1
    lax.fori_loop(0, (n_past + per_trip - 1) // per_trip, body, (*first_maxes, *ms))
    for t in range(tiles):
        acc0, acc1 = acc_ref[2 * t], acc_ref[2 * t + 1]
        o_t = _cat([acc0[:HEAD_DIM] / acc0[HEAD_DIM:HEAD_DIM + 1],
                    acc1[:HEAD_DIM] / acc1[HEAD_DIM:HEAD_DIM + 1]], 0)
        o_ref[0, 0, t * L:(t + 1) * L, :] = o_t.T.astype(o_ref.dtype)


def _moba_attn(sfeat, slopes, q, k, v, unroll, tiles):
    batch, _, seq, _ = q.shape
    nb = seq // MOBA_BLOCK
    assert nb % tiles == 0
    tile = pl.BlockSpec((1, 1, tiles * MOBA_BLOCK, LANES), lambda b, p, n: (b, p, n, 0))
    full = pl.BlockSpec((1, 1, seq, LANES), lambda b, p, n: (b, p, 0, 0))
    smem = pl.BlockSpec(memory_space=pltpu.SMEM)
    return pl.pallas_call(
        functools.partial(_moba_attn_kernel, nb=nb, unroll=unroll, tiles=tiles),
        grid=(batch, N_PAIRS, nb // tiles),
        in_specs=[smem, smem, tile, full, full],
        out_specs=tile,
        out_shape=jax.ShapeDtypeStruct(q.shape, BF16),
        scratch_shapes=[pltpu.VMEM((nb, LANES), F32),
                        pltpu.VMEM((2, seq, LANES), BF16),
                        pltpu.VMEM((nb, 2, V_ROWS, MOBA_BLOCK), BF16),
                        pltpu.VMEM((2, 2 * tiles * unroll, MOBA_BLOCK, MOBA_BLOCK), F32),
                        pltpu.VMEM((2 * tiles, V_ROWS, MOBA_BLOCK), F32)],
        compiler_params=_params(3),
        name="moba_attn",
    )(sfeat, slopes, q, k, v)


def _out_proj_kernel(o_ref, sg_ref, x_ref, w_ref, g_ref, y_ref, *, final_norm):
    o = _cat([o_ref[0, p] for p in range(N_PAIRS)], 1)
    gated = (o.astype(F32) * sg_ref[...].astype(F32)).astype(BF16)
    y = x_ref[...] + _dot(gated, w_ref[...])
    if final_norm:
        y = _rms(y, g_ref[...])
    y_ref[...] = y


def _out_proj(o_pairs, sg, x2d, w_bf16, norm_g, seq, tm, final_norm):
    m = x2d.shape[0]
    nsb = seq // tm
    row_spec = pl.BlockSpec((tm, D_MODEL), lambda i: (i, 0))
    return pl.pallas_call(
        functools.partial(_out_proj_kernel, final_norm=final_norm),
        grid=(m // tm,),
        in_specs=[
            pl.BlockSpec((1, N_PAIRS, tm, LANES), lambda i: (i // nsb, 0, i % nsb, 0)),
            row_spec, row_spec,
            _const_spec((WIDTH, D_MODEL)),
            _const_spec((1, D_MODEL)),
        ],
        out_specs=row_spec,
        out_shape=jax.ShapeDtypeStruct((m, D_MODEL), F32),
        compiler_params=_params(1),
        name="out_proj_final" if final_norm else "out_proj",
    )(o_pairs, sg, x2d, w_bf16, norm_g)


def _rwkv_proj_kernel(x_ref, halo_ref, g_ref, mix_ref, win_ref, w0_ref, w1t_ref, w2_ref,
                      a0_ref, a1t_ref, a2_ref, kk_ref, ka_ref, rk_ref,
                      r_out, k_out, v_out, lw_out, kk_out, bb_out, bonus_out, sg_out,
                      *, tiles_per_seq):
    i = pl.program_id(0)
    tm = x_ref.shape[0]
    g = g_ref[...]
    h = _rms(x_ref[...], g)
    prev_row = _rms(halo_ref[...], g)[7:8, :]
    prev_row = jnp.where(i % tiles_per_seq == 0, 0.0, prev_row)
    rolled = pltpu.roll(h, 1, 0)
    h_prev = jnp.where(_iota((tm, D_MODEL), 0) == 0, prev_row, rolled)
    xx = h_prev - h

    def stream(n):
        return h + xx * mix_ref[n:n + 1, :]

    r = _dot(stream(0).astype(BF16), win_ref[0])
    k = _dot(stream(1).astype(BF16), win_ref[1])
    v = _dot(stream(2).astype(BF16), win_ref[2])
    gt = _dot(stream(3).astype(BF16), win_ref[3])
    sg_out[...] = (gt * jax.nn.sigmoid(gt)).astype(BF16)

    def lora(xs, down_t_ref, up_ref, act):
        mid_t = act(_dot_nt(down_t_ref[...], xs.astype(BF16)))
        return _dot_tn(mid_t.astype(BF16), up_ref[...])

    z = -(w0_ref[...] + lora(stream(4), w1t_ref, w2_ref, jnp.tanh))
    softplus = jnp.maximum(z, 0.0) + jnp.log(1.0 + jnp.exp(-jnp.abs(z)))
    lw = -jnp.exp(-softplus - 0.5)
    a = jax.nn.sigmoid(a0_ref[...] + lora(stream(5), a1t_ref, a2_ref, lambda t: t))
    kr = k * kk_ref[...]
    k_mod = k * (1.0 + (a - 1.0) * ka_ref[...])
    rk = r * k_mod * rk_ref[...]
    ones_bd = _head_blockdiag(LANES).astype(BF16)
    for p in range(N_PAIRS):
        cols = slice(p * LANES, (p + 1) * LANES)
        kr_p = kr[:, cols]
        ss = _dot((kr_p * kr_p).astype(BF16), ones_bd)
        kk_p = kr_p * jnp.minimum(lax.rsqrt(ss), 1e12)
        r_out[0, p] = r[:, cols]
        k_out[0, p] = k_mod[:, cols]
        v_out[0, p] = v[:, cols]
        lw_out[0, p] = lw[:, cols]
        kk_out[0, p] = kk_p
        bb_out[0, p] = kk_p * a[:, cols]
        bonus_out[0, p] = _dot(rk[:, cols].astype(BF16), ones_bd) * v[:, cols]


def _rwkv_proj(x2d, weights, batch, seq, tm):
    m = x2d.shape[0]
    nsb = seq // tm
    halo_blocks = tm // 8
    pair_spec = pl.BlockSpec((1, N_PAIRS, tm, LANES), lambda i: (i // nsb, 0, i % nsb, 0))
    pair_shape = jax.ShapeDtypeStruct((batch, N_PAIRS, seq, LANES), F32)
    vec = _const_spec((1, WIDTH))
    lora_w = _const_spec((LORA, WIDTH))
    return pl.pallas_call(
        functools.partial(_rwkv_proj_kernel, tiles_per_seq=nsb),
        grid=(m // tm,),
        in_specs=[
            pl.BlockSpec((tm, D_MODEL), lambda i: (i, 0)),
            pl.BlockSpec((8, D_MODEL), lambda i: (jnp.maximum(i * halo_blocks - 1, 0), 0)),
            _const_spec((1, D_MODEL)),
            _const_spec((6, D_MODEL)),
            _const_spec((4, D_MODEL, WIDTH)),
            vec, lora_w, lora_w,
            vec, lora_w, lora_w,
            vec, vec, vec,
        ],
        out_specs=[pair_spec] * 7 + [pl.BlockSpec((tm, WIDTH), lambda i: (i, 0))],
        out_shape=[pair_shape] * 7 + [jax.ShapeDtypeStruct((m, WIDTH), BF16)],
        compiler_params=_params(1),
        name="rwkv_proj",
    )(x2d, x2d, *weights)


def _rwkv_scan_kernel(r_ref, k_ref, v_ref, lw_ref, kk_ref, bb_ref, bonus_ref, lng_ref, lnb_ref,
                      o_ref, z_ref, *, chunks_per_step):
    C = CHUNK
    pairs = range(N_PAIRS)

    @pl.when(pl.program_id(1) == 0)
    def _():
        z_ref[...] = jnp.zeros_like(z_ref)

    row = _iota((C, LANES), 0)
    lane = _iota((C, LANES), 1)
    col = lane % C
    head_lo = lane < HEAD_DIM
    keep_lo = head_lo.astype(BF16)
    keep_hi = jnp.logical_not(head_lo).astype(BF16)
    strict = col < row
    row4 = _iota((C, 2 * LANES), 0)
    lane4 = _iota((C, 2 * LANES), 1)
    col4 = lane4 % C
    lower2 = col4 <= row4
    eye = (row4 == col4).astype(F32)
    same_block = {b: (row4 // b) == (col4 // b) for b in (8, 16, 32)}
    keep_head = [(lane4 // HEAD_DIM == h).astype(BF16) for h in range(4)]
    tri3 = ((_iota((C, 3 * C), 1) % C) <= _iota((C, 3 * C), 0)).astype(BF16)
    head_bd = _head_blockdiag(LANES)
    avg_bd = head_bd.astype(BF16) * (1.0 / HEAD_DIM)
    avg_bd2 = _cat([avg_bd, avg_bd], 0)

    def blockdiag(xb):
        return _cat([xb * keep_lo, xb * keep_hi], 0)

    def prod(a_list, b_list):
        parts = [(a.astype(BF16), b.astype(BF16)) for a, b in zip(a_list, b_list)]
        return [_dot(ab, blockdiag(bb)) for ab, bb in parts]

    def prod_bf16(a_list, b_list):
        parts = [(a.astype(BF16), b.astype(BF16)) for a, b in zip(a_list, b_list)]
        return [_dot(ab, _cat([bb * keep for keep in keep_head], 0)) for ab, bb in parts]

    def state_free(c, out):
        rows = slice(c * C, (c + 1) * C)
        v = [v_ref[0, p, rows, :] for p in pairs]
        lw = [lw_ref[0, p, rows, :] for p in pairs]
        lw3 = [_cat(_split3(x), 0) for x in lw]
        cs = [_dot(tri3, x) for x in lw3]
        yield
        decay_in = [jnp.exp(x) for x in cs]
        inv = [jnp.exp(-x) for x in cs]
        r_t = [r_ref[0, p, rows, :] * decay_in[p] for p in pairs]
        k_t = [k_ref[0, p, rows, :] * inv[p] for p in pairs]
        b_t = [bb_ref[0, p, rows, :] * inv[p] for p in pairs]
        a_t = [-(kk_ref[0, p, rows, :] * jnp.exp(cs[p] - lw[p])) for p in pairs]
        ar = [_cat([a_t[p], r_t[p]], 0).astype(BF16) for p in pairs]
        bk_rows = [_cat([blockdiag(b_t[p].astype(BF16)), blockdiag(k_t[p].astype(BF16))], 0)
                   for p in pairs]
        mbk = [_dot_nt(ar[p], bk_rows[p]) for p in pairs]
        yield
        low = [jnp.where(strict, x[:C, :LANES], 0.0) for x in mbk]
        ak = [jnp.where(strict, x[:C, LANES:], 0.0) for x in mbk]
        rbk = [jnp.where(lower2, x[C:], 0.0) for x in mbk]

        quads = range(N_PAIRS // 2)
        low4 = [_cat([low[2 * i], low[2 * i + 1]], 1) for i in quads]
        ld = [jnp.where(same_block[8], x, 0.0) for x in low4]
        l2 = prod_bf16(ld, ld)
        yield
        l4 = prod_bf16(l2, l2)
        l3 = prod_bf16(ld, l2)
        yield
        p1 = [eye + ld[i] + l2[i] + l3[i] for i in quads]
        p1l4 = prod_bf16(p1, l4)
        yield
        t4 = [p1[i] + p1l4[i] for i in quads]
        b = 8
        while b < C:
            couple = jnp.logical_not(same_block[b])
            if 2 * b < C:
                couple = jnp.logical_and(same_block[2 * b], couple)
            x = [jnp.where(couple, y, 0.0) for y in low4]
            tx = prod_bf16(t4, x)
            yield
            txt = prod_bf16(tx, t4)
            yield
            t4 = [t4[i] + txt[i] for i in quads]
            b *= 2
        t = [t4[p // 2][:, (p % 2) * LANES:(p % 2 + 1) * LANES] for p in pairs]
        akv = prod(ak, v)
        yield
        scale = [x[C - 1:C, :] for x in decay_in]
        out.update(
            v=v, ar=ar, rsp=[x.astype(BF16) for x in rbk], t=t, akv=akv,
            bk=[_cat([b_t[p] * scale[p], k_t[p] * scale[p]], 0).astype(BF16) for p in pairs],
            decay_cols=[x[C - 8:C, :].T[:, 7:8] for x in decay_in])

    def state_step(c, pre, z):
        rows = slice(c * C, (c + 1) * C)
        st = [_dot(pre["ar"][p], z[p].astype(BF16)) for p in pairs]
        yield
        rhs = [st[p][:C] + pre["akv"][p] for p in pairs]
        u = prod(pre["t"], rhs)
        yield
        ub = [x.astype(BF16) for x in u]
        vb = [x.astype(BF16) for x in pre["v"]]
        uv = [_cat([blockdiag(ub[p]), blockdiag(vb[p])], 0) for p in pairs]
        y = [st[p][C:] + _dot(pre["rsp"][p], uv[p]) for p in pairs]
        upd = [_dot_tn(pre["bk"][p], _cat([ub[p], vb[p]], 0)) for p in pairs]
        yield
        for p in pairs:
            z[p] = pre["decay_cols"][p] * z[p] + jnp.where(head_bd, upd[p], 0.0)
        mu = [_mm2(x, avg_bd2) for x in y]
        yield
        d = [y[p] - mu[p] for p in pairs]
        var = [_mm2(x * x, avg_bd2) for x in d]
        yield
        for p in pairs:
            yn = d[p] * lax.rsqrt(var[p] + LNX_EPS) * lng_ref[p] + lnb_ref[p]
            o_ref[0, p, rows, :] = (yn + bonus_ref[0, p, rows, :]).astype(o_ref.dtype)

    z = [z_ref[p] for p in pairs]
    prepared, running = {}, {}

    def start_more():
        while len(running) < SCAN_CHUNKS_AHEAD and len(prepared) < chunks_per_step:
            c_new = len(prepared)
            prepared[c_new] = {}
            running[c_new] = state_free(c_new, prepared[c_new])

    def advance():
        for c_run in list(running):
            if next(running[c_run], "done") == "done":
                del running[c_run]
        start_more()

    start_more()
    for c in range(chunks_per_step):
        while c in running:
            advance()
        for _ in state_step(c, prepared[c], z):
            advance()
    for p in pairs:
        z_ref[p] = z[p]


def _rwkv_scan(r, k, v, lw, kk, bb, bonus, lnx_g, lnx_b, chunks_per_step):
    batch, _, seq, _ = r.shape
    tc = CHUNK * chunks_per_step
    blk = pl.BlockSpec((1, N_PAIRS, tc, LANES), lambda b, c: (b, 0, c, 0))
    vec = _const_spec((N_PAIRS, 1, LANES))
    return pl.pallas_call(
        functools.partial(_rwkv_scan_kernel, chunks_per_step=chunks_per_step),
        grid=(batch, seq // tc),
        in_specs=[blk] * 7 + [vec, vec],
        out_specs=blk,
        out_shape=jax.ShapeDtypeStruct(r.shape, BF16),
        scratch_shapes=[pltpu.VMEM((N_PAIRS, LANES, LANES), F32)],
        compiler_params=_params(2),
        name="rwkv_scan",
    )(r, k, v, lw, kk, bb, bonus, lnx_g, lnx_b)


def kernel(x, moba_norm_g, moba_w_in, moba_w_out, rwkv_norm_g, rwkv_mix, rwkv_w_in, rwkv_w0, rwkv_w1, rwkv_w2, rwkv_a0, rwkv_a1, rwkv_a2, rwkv_k_k, rwkv_k_a, rwkv_r_k, rwkv_lnx_g, rwkv_lnx_b, rwkv_w_out, final_norm_g):
    batch, seq, d = x.shape
    nb = seq // MOBA_BLOCK
    assert d == D_MODEL and seq % MOBA_BLOCK == 0
    assert nb <= 32, "the chosen-block mask of a query is one int32"
    assert nb % ATTN_TILES == 0 and nb % ATTN_BLOCKS_PER_GROUP == 0
    assert seq % PROJ_ROWS == 0 and seq % (CHUNK * SCAN_CHUNKS_PER_STEP) == 0
    x2d = x.reshape(batch * seq, d)
    row = lambda t: t.reshape(1, -1).astype(F32)

    q, k, v, sg = _moba_proj(x2d, row(moba_norm_g), moba_w_in.astype(BF16), batch, seq,
                             tm=PROJ_ROWS)
    slopes = jnp.asarray([2.0 ** (-8.0 * (i + 1) / N_HEADS) for i in range(N_HEADS)], F32)
    slope_hi = (slopes * LOG2E).astype(BF16).astype(F32)
    sfeat = jnp.stack([slope_hi, slopes * LOG2E - slope_hi], axis=1).reshape(-1)
    o = _moba_attn(sfeat, slopes, q, k, v, unroll=ATTN_BLOCKS_PER_GROUP, tiles=ATTN_TILES)
    x1 = _out_proj(o, sg, x2d, moba_w_out.astype(BF16), row(final_norm_g), seq, tm=PROJ_ROWS,
                   final_norm=False)

    weights = (row(rwkv_norm_g), rwkv_mix.astype(F32), rwkv_w_in.astype(BF16),
               row(rwkv_w0), rwkv_w1.T.astype(BF16), rwkv_w2.astype(BF16),
               row(rwkv_a0), rwkv_a1.T.astype(BF16), rwkv_a2.astype(BF16),
               row(rwkv_k_k), row(rwkv_k_a), row(rwkv_r_k))
    r, k2, v2, lw, kk, bb, bonus, sg2 = _rwkv_proj(x1, weights, batch, seq, tm=PROJ_ROWS)
    pair_vec = lambda t: t.reshape(N_PAIRS, 1, LANES).astype(F32)
    o2 = _rwkv_scan(r, k2, v2, lw, kk, bb, bonus, pair_vec(rwkv_lnx_g), pair_vec(rwkv_lnx_b),
                    chunks_per_step=SCAN_CHUNKS_PER_STEP)
    out = _out_proj(o2, sg2, x1, rwkv_w_out.astype(BF16), row(final_norm_g), seq, tm=PROJ_ROWS,
                    final_norm=True)
    return out.reshape(batch, seq, d)
```

```python
import functools
import math

import jax
import jax.numpy as jnp
from jax import lax
from jax.experimental import pallas as pl
from jax.experimental.pallas import tpu as pltpu

F32 = jnp.float32
BF16 = jnp.bfloat16

D_MODEL = 1024
N_HEADS = 16
HEAD_DIM = 64
WIDTH = N_HEADS * HEAD_DIM
LANES = 128
N_PAIRS = WIDTH // LANES
MOBA_BLOCK = 256
MOBA_TOPK = 3
BF16_SUBLANES = 16
V_ROWS = HEAD_DIM + BF16_SUBLANES
LORA = 64
NORM_EPS = 1e-6
LNX_EPS = 64e-5
CHUNK = 64
SCAN_CHUNKS_AHEAD = 2
SCAN_CHUNKS_PER_STEP = 8
PROJ_ROWS = 512
ATTN_TILES = 4
ATTN_BLOCKS_PER_GROUP = 2
VMEM_LIMIT = 56 * 1024 * 1024
NEG_INF = float("-inf")
POS_INF = float("inf")
LOG2E = math.log2(math.e)


def _dot(a, b):
    return jnp.dot(a, b, preferred_element_type=F32)


def _dot_nt(a, b):
    return lax.dot_general(a, b, (((1,), (1,)), ((), ())), preferred_element_type=F32)


def _dot_tn(a, b):
    return lax.dot_general(a, b, (((0,), (0,)), ((), ())), preferred_element_type=F32)


def _split2(x):
    hi = x.astype(BF16)
    lo = (x - hi.astype(F32)).astype(BF16)
    return hi, lo


def _split3(x):
    hi = x.astype(BF16)
    r1 = x - hi.astype(F32)
    mid = r1.astype(BF16)
    lo = (r1 - mid.astype(F32)).astype(BF16)
    return hi, mid, lo


def _cat(xs, axis):
    return jnp.concatenate(xs, axis=axis)


def _mm3_parts(ah, al, bh, bl, kind):
    if kind == "nn":
        return _dot(_cat([ah, al], 1), _cat([bh, bh], 0)) + _dot(ah, bl)
    if kind == "nt":
        return _dot_nt(_cat([ah, al], 1), _cat([bh, bh], 1)) + _dot_nt(ah, bl)
    assert kind == "tn"
    return _dot_tn(_cat([ah, al], 0), _cat([bh, bh], 0)) + _dot_tn(ah, bl)


def _mm3(a, b, kind="nn"):
    ah, al = _split2(a)
    bh, bl = _split2(b)
    return _mm3_parts(ah, al, bh, bl, kind)


def _mm2(a, b2_exact):
    ah, al = _split2(a)
    return _dot(_cat([ah, al], 1), b2_exact)


def _rms(x, g):
    ms = jnp.mean(x * x, axis=-1, keepdims=True)
    return x * lax.rsqrt(ms + NORM_EPS) * g


def _iota(shape, dim):
    return lax.broadcasted_iota(jnp.int32, shape, dim)


def _head_blockdiag(n):
    return (_iota((n, n), 0) // HEAD_DIM) == (_iota((n, n), 1) // HEAD_DIM)


def _const_spec(shape):
    nd = len(shape)
    return pl.BlockSpec(shape, lambda *_: (0,) * nd)


def _params(n_axes):
    return pltpu.CompilerParams(dimension_semantics=("arbitrary",) * n_axes,
                                vmem_limit_bytes=VMEM_LIMIT)


def _moba_proj_kernel(x_ref, g_ref, w_ref, q_ref, k_ref, v_ref, sg_ref):
    h = _rms(x_ref[...], g_ref[...]).astype(BF16)
    q = _dot(h, w_ref[:, 0 * WIDTH:1 * WIDTH])
    k = _dot(h, w_ref[:, 1 * WIDTH:2 * WIDTH])
    v = _dot(h, w_ref[:, 2 * WIDTH:3 * WIDTH])
    gate = _dot(h, w_ref[:, 3 * WIDTH:4 * WIDTH])
    for p in range(N_PAIRS):
        cols = slice(p * LANES, (p + 1) * LANES)
        q_ref[0, p] = q[:, cols]
        k_ref[0, p] = k[:, cols].astype(BF16)
        v_ref[0, p] = v[:, cols].astype(BF16)
    sg_ref[...] = (gate * jax.nn.sigmoid(gate)).astype(BF16)


def _moba_proj(x2d, norm_g, w_bf16, batch, seq, tm):
    m = x2d.shape[0]
    nsb = seq // tm
    pair_spec = pl.BlockSpec((1, N_PAIRS, tm, LANES), lambda i: (i // nsb, 0, i % nsb, 0))
    pair_shape = (batch, N_PAIRS, seq, LANES)
    return pl.pallas_call(
        _moba_proj_kernel,
        grid=(m // tm,),
        in_specs=[
            pl.BlockSpec((tm, D_MODEL), lambda i: (i, 0)),
            _const_spec((1, D_MODEL)),
            _const_spec((D_MODEL, 4 * WIDTH)),
        ],
        out_specs=[pair_spec, pair_spec, pair_spec,
                   pl.BlockSpec((tm, WIDTH), lambda i: (i, 0))],
        out_shape=[
            jax.ShapeDtypeStruct(pair_shape, F32),
            jax.ShapeDtypeStruct(pair_shape, BF16),
            jax.ShapeDtypeStruct(pair_shape, BF16),
            jax.ShapeDtypeStruct((m, WIDTH), BF16),
        ],
        compiler_params=_params(1),
        name="moba_proj",
    )(x2d, norm_g, w_bf16)


def _moba_attn_kernel(sfeat_ref, slopes_ref, q_ref, k_ref, v_ref, o_ref,
                      kmean_ref, kaug_ref, vt_ref, s_ref, acc_ref, *, nb, unroll, tiles):
    L = MOBA_BLOCK
    p = pl.program_id(1)
    n = pl.program_id(2)
    lane = _iota((L, LANES), 1)
    head_lo = lane < HEAD_DIM
    head_masks = (head_lo, jnp.logical_not(head_lo))
    feat_base = (HEAD_DIM, 0)

    @pl.when(n == 0)
    def _():
        offs = _iota((L, LANES), 0).astype(F32)
        ones_row = (_iota((V_ROWS - HEAD_DIM, L), 0) == 0).astype(BF16)
        for j in range(nb):
            rows = slice(j * L, (j + 1) * L)
            kf = k_ref[0, 0, rows, :].astype(F32)
            kmean_ref[j:j + 1, :] = jnp.mean(kf, axis=0, keepdims=True)
            for hh in range(2):
                fb = feat_base[hh]
                feat = jnp.where(jnp.logical_or(lane == fb, lane == fb + 1), offs, 0.0)
                kaug_ref[hh, rows, :] = jnp.where(head_masks[hh], kf, feat).astype(BF16)
            v_t = v_ref[0, 0, rows, :].astype(F32).T.astype(BF16)
            for hh in range(2):
                vt_ref[j, hh, :HEAD_DIM, :] = v_t[hh * HEAD_DIM:(hh + 1) * HEAD_DIM, :]
                vt_ref[j, hh, HEAD_DIM:, :] = ones_row

    n_first = n * tiles
    units = [(t, hh) for t in range(tiles) for hh in range(2)]
    blk = _iota((nb, L), 0)
    kmean = kmean_ref[...]
    key_i = _iota((L, L), 0)
    qry_i = _iota((L, L), 1)

    qa, bits = [], []
    for t, hh in units:
        h = 2 * p + hh
        qf = q_ref[0, 0, t * L:(t + 1) * L, :]
        gate = _mm3(kmean, jnp.where(head_masks[hh], qf, 0.0), "nt")
        gate = jnp.where(blk < n_first + t, gate, NEG_INF)
        chosen_bits = jnp.zeros((1, L), jnp.int32)
        for _ in range(MOBA_TOPK):
            mx = jnp.max(gate, axis=0, keepdims=True)
            first = jnp.min(jnp.where(gate == mx, blk, nb), axis=0, keepdims=True)
            valid = mx > NEG_INF
            chosen_bits = chosen_bits | jnp.where(valid, jnp.left_shift(1, first), 0)
            gate = jnp.where(jnp.logical_and(blk == first, valid), NEG_INF, gate)
        bits.append(chosen_bits)
        fb = feat_base[hh]
        feat = jnp.where(lane == fb, sfeat_ref[2 * h],
                         jnp.where(lane == fb + 1, sfeat_ref[2 * h + 1], 0.0))
        qa.append(jnp.where(head_masks[hh], qf * (HEAD_DIM ** -0.5 * LOG2E), feat).astype(BF16))
    qa_all = [_cat([qa[2 * t + hh] for t in range(tiles)], 0) for hh in range(2)]

    nu = len(units)

    def issue_scores(g, slot):
        off = pl.multiple_of(jnp.minimum(g * unroll, nb - unroll) * L, L)
        per_head = [_dot_nt(kaug_ref[hh, pl.ds(off, unroll * L), :], qa_all[hh])
                    for hh in range(2)]
        maxes = []
        for u in range(unroll):
            for w, (t, hh) in enumerate(units):
                s = per_head[hh][u * L:(u + 1) * L, t * L:(t + 1) * L]
                s_ref[slot, nu * u + w] = s
                maxes.append(jnp.max(s, axis=0, keepdims=True))
        return maxes

    def consume(g, slot, maxes, ms):
        ms = list(ms)
        base = jnp.minimum(g * unroll, nb - unroll)
        for w, (t, hh) in enumerate(units):
            tile = n_first + t
            cs, chosens = [], []
            m_new = ms[w]
            for u in range(unroll):
                j = g * unroll + u
                jc = jnp.minimum(j, nb - 1)
                c = ((j - tile) * L).astype(F32) * LOG2E * slopes_ref[2 * p + hh]
                chosen = jnp.logical_and((jnp.right_shift(bits[w], jc) & 1) == 1, j < tile)
                m_new = jnp.maximum(m_new, jnp.where(chosen, maxes[nu * u + w] + c, NEG_INF))
                cs.append(c)
                chosens.append(chosen)
            alpha = jnp.exp2(ms[w] - m_new)
            prs = [jnp.exp2(s_ref[slot, nu * u + w]
                            - jnp.where(chosens[u], m_new - cs[u], POS_INF)).astype(BF16)
                   for u in range(unroll)]
            v_t = _cat([vt_ref[base + u, hh] for u in range(unroll)], 1)
            acc_ref[w] = alpha * acc_ref[w] + _dot(v_t, _cat(prs, 0))
            ms[w] = m_new
        return ms

    own = [_dot_nt(kaug_ref[hh, pl.ds(pl.multiple_of((n_first + t) * L, L), L), :], qa[2 * t + hh])
           for t, hh in units]
    first_maxes = issue_scores(0, 0)
    ms = []
    for w, (t, hh) in enumerate(units):
        tile = n_first + t
        s = jnp.where(key_i <= qry_i, own[w], NEG_INF)
        m = jnp.max(s, axis=0, keepdims=True)
        pr = jnp.exp2(s - m).astype(BF16)
        acc_ref[w] = _dot(vt_ref[tile, hh], pr)
        ms.append(m)

    nm = nu * unroll

    def body(i, carry):
        maxes, ms = carry[:nm], carry[nm:]
        maxes_b = issue_scores(2 * i + 1, 1)
        ms = consume(2 * i, 0, maxes, ms)
        maxes_a = issue_scores(2 * i + 2, 0)
        ms = consume(2 * i + 1, 1, maxes_b, ms)
        return (*maxes_a, *ms)

    per_trip = 2 * unroll
    n_past = n_first + tiles - 1
    lax.fori_loop(0, (n_past + per_trip - 1) // per_trip, body, (*first_maxes, *ms))
    for t in range(tiles):
        acc0, acc1 = acc_ref[2 * t], acc_ref[2 * t + 1]
        o_t = _cat([acc0[:HEAD_DIM] / acc0[HEAD_DIM:HEAD_DIM + 1],
                    acc1[:HEAD_DIM] / acc1[HEAD_DIM:HEAD_DIM + 1]], 0)
        o_ref[0, 0, t * L:(t + 1) * L, :] = o_t.T.astype(o_ref.dtype)


def _moba_attn(sfeat, slopes, q, k, v, unroll, tiles):
    batch, _, seq, _ = q.shape
    nb = seq // MOBA_BLOCK
    assert nb % tiles == 0
    tile = pl.BlockSpec((1, 1, tiles * MOBA_BLOCK, LANES), lambda b, p, n: (b, p, n, 0))
    full = pl.BlockSpec((1, 1, seq, LANES), lambda b, p, n: (b, p, 0, 0))
    smem = pl.BlockSpec(memory_space=pltpu.SMEM)
    return pl.pallas_call(
        functools.partial(_moba_attn_kernel, nb=nb, unroll=unroll, tiles=tiles),
        grid=(batch, N_PAIRS, nb // tiles),
        in_specs=[smem, smem, tile, full, full],
        out_specs=tile,
        out_shape=jax.ShapeDtypeStruct(q.shape, BF16),
        scratch_shapes=[pltpu.VMEM((nb, LANES), F32),
                        pltpu.VMEM((2, seq, LANES), BF16),
                        pltpu.VMEM((nb, 2, V_ROWS, MOBA_BLOCK), BF16),
                        pltpu.VMEM((2, 2 * tiles * unroll, MOBA_BLOCK, MOBA_BLOCK), F32),
                        pltpu.VMEM((2 * tiles, V_ROWS, MOBA_BLOCK), F32)],
        compiler_params=_params(3),
        name="moba_attn",
    )(sfeat, slopes, q, k, v)


def _out_proj_kernel(o_ref, sg_ref, x_ref, w_ref, g_ref, y_ref, *, final_norm):
    o = _cat([o_ref[0, p] for p in range(N_PAIRS)], 1)
    gated = (o.astype(F32) * sg_ref[...].astype(F32)).astype(BF16)
    y = x_ref[...] + _dot(gated, w_ref[...])
    if final_norm:
        y = _rms(y, g_ref[...])
    y_ref[...] = y


def _out_proj(o_pairs, sg, x2d, w_bf16, norm_g, seq, tm, final_norm):
    m = x2d.shape[0]
    nsb = seq // tm
    row_spec = pl.BlockSpec((tm, D_MODEL), lambda i: (i, 0))
    return pl.pallas_call(
        functools.partial(_out_proj_kernel, final_norm=final_norm),
        grid=(m // tm,),
        in_specs=[
            pl.BlockSpec((1, N_PAIRS, tm, LANES), lambda i: (i // nsb, 0, i % nsb, 0)),
            row_spec, row_spec,
            _const_spec((WIDTH, D_MODEL)),
            _const_spec((1, D_MODEL)),
        ],
        out_specs=row_spec,
        out_shape=jax.ShapeDtypeStruct((m, D_MODEL), F32),
        compiler_params=_params(1),
        name="out_proj_final" if final_norm else "out_proj",
    )(o_pairs, sg, x2d, w_bf16, norm_g)


def _rwkv_proj_kernel(x_ref, halo_ref, g_ref, mix_ref, win_ref, w0_ref, w1t_ref, w2_ref,
                      a0_ref, a1t_ref, a2_ref, kk_ref, ka_ref, rk_ref,
                      r_out, k_out, v_out, lw_out, kk_out, bb_out, bonus_out, sg_out,
                      *, tiles_per_seq):
    i = pl.program_id(0)
    tm = x_ref.shape[0]
    g = g_ref[...]
    h = _rms(x_ref[...], g)
    prev_row = _rms(halo_ref[...], g)[7:8, :]
    prev_row = jnp.where(i % tiles_per_seq == 0, 0.0, prev_row)
    rolled = pltpu.roll(h, 1, 0)
    h_prev = jnp.where(_iota((tm, D_MODEL), 0) == 0, prev_row, rolled)
    xx = h_prev - h

    def stream(n):
        return h + xx * mix_ref[n:n + 1, :]

    r = _dot(stream(0).astype(BF16), win_ref[0])
    k = _dot(stream(1).astype(BF16), win_ref[1])
    v = _dot(stream(2).astype(BF16), win_ref[2])
    gt = _dot(stream(3).astype(BF16), win_ref[3])
    sg_out[...] = (gt * jax.nn.sigmoid(gt)).astype(BF16)

    def lora(xs, down_t_ref, up_ref, act):
        mid_t = act(_dot_nt(down_t_ref[...], xs.astype(BF16)))
        return _dot_tn(mid_t.astype(BF16), up_ref[...])

    z = -(w0_ref[...] + lora(stream(4), w1t_ref, w2_ref, jnp.tanh))
    softplus = jnp.maximum(z, 0.0) + jnp.log(1.0 + jnp.exp(-jnp.abs(z)))
    lw = -jnp.exp(-softplus - 0.5)
    a = jax.nn.sigmoid(a0_ref[...] + lora(stream(5), a1t_ref, a2_ref, lambda t: t))
    kr = k * kk_ref[...]
    k_mod = k * (1.0 + (a - 1.0) * ka_ref[...])
    rk = r * k_mod * rk_ref[...]
    ones_bd = _head_blockdiag(LANES).astype(BF16)
    for p in range(N_PAIRS):
        cols = slice(p * LANES, (p + 1) * LANES)
        kr_p = kr[:, cols]
        ss = _dot((kr_p * kr_p).astype(BF16), ones_bd)
        kk_p = kr_p * jnp.minimum(lax.rsqrt(ss), 1e12)
        r_out[0, p] = r[:, cols]
        k_out[0, p] = k_mod[:, cols]
        v_out[0, p] = v[:, cols]
        lw_out[0, p] = lw[:, cols]
        kk_out[0, p] = kk_p
        bb_out[0, p] = kk_p * a[:, cols]
        bonus_out[0, p] = _dot(rk[:, cols].astype(BF16), ones_bd) * v[:, cols]


def _rwkv_proj(x2d, weights, batch, seq, tm):
    m = x2d.shape[0]
    nsb = seq // tm
    halo_blocks = tm // 8
    pair_spec = pl.BlockSpec((1, N_PAIRS, tm, LANES), lambda i: (i // nsb, 0, i % nsb, 0))
    pair_shape = jax.ShapeDtypeStruct((batch, N_PAIRS, seq, LANES), F32)
    vec = _const_spec((1, WIDTH))
    lora_w = _const_spec((LORA, WIDTH))
    return pl.pallas_call(
        functools.partial(_rwkv_proj_kernel, tiles_per_seq=nsb),
        grid=(m // tm,),
        in_specs=[
            pl.BlockSpec((tm, D_MODEL), lambda i: (i, 0)),
            pl.BlockSpec((8, D_MODEL), lambda i: (jnp.maximum(i * halo_blocks - 1, 0), 0)),
            _const_spec((1, D_MODEL)),
            _const_spec((6, D_MODEL)),
            _const_spec((4, D_MODEL, WIDTH)),
            vec, lora_w, lora_w,
            vec, lora_w, lora_w,
            vec, vec, vec,
        ],
        out_specs=[pair_spec] * 7 + [pl.BlockSpec((tm, WIDTH), lambda i: (i, 0))],
        out_shape=[pair_shape] * 7 + [jax.ShapeDtypeStruct((m, WIDTH), BF16)],
        compiler_params=_params(1),
        name="rwkv_proj",
    )(x2d, x2d, *weights)


def _rwkv_scan_kernel(r_ref, k_ref, v_ref, lw_ref, kk_ref, bb_ref, bonus_ref, lng_ref, lnb_ref,
                      o_ref, z_ref, *, chunks_per_step):
    C = CHUNK
    pairs = range(N_PAIRS)

    @pl.when(pl.program_id(1) == 0)
    def _():
        z_ref[...] = jnp.zeros_like(z_ref)

    row = _iota((C, LANES), 0)
    lane = _iota((C, LANES), 1)
    col = lane % C
    head_lo = lane < HEAD_DIM
    keep_lo = head_lo.astype(BF16)
    keep_hi = jnp.logical_not(head_lo).astype(BF16)
    strict = col < row
    row4 = _iota((C, 2 * LANES), 0)
    lane4 = _iota((C, 2 * LANES), 1)
    col4 = lane4 % C
    lower2 = col4 <= row4
    eye = (row4 == col4).astype(F32)
    same_block = {b: (row4 // b) == (col4 // b) for b in (8, 16, 32)}
    keep_head = [(lane4 // HEAD_DIM == h).astype(BF16) for h in range(4)]
    tri3 = ((_iota((C, 3 * C), 1) % C) <= _iota((C, 3 * C), 0)).astype(BF16)
    head_bd = _head_blockdiag(LANES)
    avg_bd = head_bd.astype(BF16) * (1.0 / HEAD_DIM)
    avg_bd2 = _cat([avg_bd, avg_bd], 0)

    def blockdiag(xb):
        return _cat([xb * keep_lo, xb * keep_hi], 0)

    def prod(a_list, b_list):
        parts = [(a.astype(BF16), b.astype(BF16)) for a, b in zip(a_list, b_list)]
        return [_dot(ab, blockdiag(bb)) for ab, bb in parts]

    def prod_bf16(a_list, b_list):
        parts = [(a.astype(BF16), b.astype(BF16)) for a, b in zip(a_list, b_list)]
        return [_dot(ab, _cat([bb * keep for keep in keep_head], 0)) for ab, bb in parts]

    def state_free(c, out):
        rows = slice(c * C, (c + 1) * C)
        v = [v_ref[0, p, rows, :] for p in pairs]
        lw = [lw_ref[0, p, rows, :] for p in pairs]
        lw3 = [_cat(_split3(x), 0) for x in lw]
        cs = [_dot(tri3, x) for x in lw3]
        yield
        decay_in = [jnp.exp(x) for x in cs]
        inv = [jnp.exp(-x) for x in cs]
        r_t = [r_ref[0, p, rows, :] * decay_in[p] for p in pairs]
        k_t = [k_ref[0, p, rows, :] * inv[p] for p in pairs]
        b_t = [bb_ref[0, p, rows, :] * inv[p] for p in pairs]
        a_t = [-(kk_ref[0, p, rows, :] * jnp.exp(cs[p] - lw[p])) for p in pairs]
        ar = [_cat([a_t[p], r_t[p]], 0).astype(BF16) for p in pairs]
        bk_rows = [_cat([blockdiag(b_t[p].astype(BF16)), blockdiag(k_t[p].astype(BF16))], 0)
                   for p in pairs]
        mbk = [_dot_nt(ar[p], bk_rows[p]) for p in pairs]
        yield
        low = [jnp.where(strict, x[:C, :LANES], 0.0) for x in mbk]
        ak = [jnp.where(strict, x[:C, LANES:], 0.0) for x in mbk]
        rbk = [jnp.where(lower2, x[C:], 0.0) for x in mbk]

        quads = range(N_PAIRS // 2)
        low4 = [_cat([low[2 * i], low[2 * i + 1]], 1) for i in quads]
        ld = [jnp.where(same_block[8], x, 0.0) for x in low4]
        l2 = prod_bf16(ld, ld)
        yield
        l4 = prod_bf16(l2, l2)
        l3 = prod_bf16(ld, l2)
        yield
        p1 = [eye + ld[i] + l2[i] + l3[i] for i in quads]
        p1l4 = prod_bf16(p1, l4)
        yield
        t4 = [p1[i] + p1l4[i] for i in quads]
        b = 8
        while b < C:
            couple = jnp.logical_not(same_block[b])
            if 2 * b < C:
                couple = jnp.logical_and(same_block[2 * b], couple)
            x = [jnp.where(couple, y, 0.0) for y in low4]
            tx = prod_bf16(t4, x)
            yield
            txt = prod_bf16(tx, t4)
            yield
            t4 = [t4[i] + txt[i] for i in quads]
            b *= 2
        t = [t4[p // 2][:, (p % 2) * LANES:(p % 2 + 1) * LANES] for p in pairs]
        akv = prod(ak, v)
        yield
        scale = [x[C - 1:C, :] for x in decay_in]
        out.update(
            v=v, ar=ar, rsp=[x.astype(BF16) for x in rbk], t=t, akv=akv,
            bk=[_cat([b_t[p] * scale[p], k_t[p] * scale[p]], 0).astype(BF16) for p in pairs],
            decay_cols=[x[C - 8:C, :].T[:, 7:8] for x in decay_in])

    def state_step(c, pre, z):
        rows = slice(c * C, (c + 1) * C)
        st = [_dot(pre["ar"][p], z[p].astype(BF16)) for p in pairs]
        yield
        rhs = [st[p][:C] + pre["akv"][p] for p in pairs]
        u = prod(pre["t"], rhs)
        yield
        ub = [x.astype(BF16) for x in u]
        vb = [x.astype(BF16) for x in pre["v"]]
        uv = [_cat([blockdiag(ub[p]), blockdiag(vb[p])], 0) for p in pairs]
        y = [st[p][C:] + _dot(pre["rsp"][p], uv[p]) for p in pairs]
        upd = [_dot_tn(pre["bk"][p], _cat([ub[p], vb[p]], 0)) for p in pairs]
        yield
        for p in pairs:
            z[p] = pre["decay_cols"][p] * z[p] + jnp.where(head_bd, upd[p], 0.0)
        mu = [_mm2(x, avg_bd2) for x in y]
        yield
        d = [y[p] - mu[p] for p in pairs]
        var = [_mm2(x * x, avg_bd2) for x in d]
        yield
        for p in pairs:
            yn = d[p] * lax.rsqrt(var[p] + LNX_EPS) * lng_ref[p] + lnb_ref[p]
            o_ref[0, p, rows, :] = (yn + bonus_ref[0, p, rows, :]).astype(o_ref.dtype)

    z = [z_ref[p] for p in pairs]
    prepared, running = {}, {}

    def start_more():
        while len(running) < SCAN_CHUNKS_AHEAD and len(prepared) < chunks_per_step:
            c_new = len(prepared)
            prepared[c_new] = {}
            running[c_new] = state_free(c_new, prepared[c_new])

    def advance():
        for c_run in list(running):
            if next(running[c_run], "done") == "done":
                del running[c_run]
        start_more()

    start_more()
    for c in range(chunks_per_step):
        while c in running:
            advance()
        for _ in state_step(c, prepared[c], z):
            advance()
    for p in pairs:
        z_ref[p] = z[p]


def _rwkv_scan(r, k, v, lw, kk, bb, bonus, lnx_g, lnx_b, chunks_per_step):
    batch, _, seq, _ = r.shape
    tc = CHUNK * chunks_per_step
    blk = pl.BlockSpec((1, N_PAIRS, tc, LANES), lambda b, c: (b, 0, c, 0))
    vec = _const_spec((N_PAIRS, 1, LANES))
    return pl.pallas_call(
        functools.partial(_rwkv_scan_kernel, chunks_per_step=chunks_per_step),
        grid=(batch, seq // tc),
        in_specs=[blk] * 7 + [vec, vec],
        out_specs=blk,
        out_shape=jax.ShapeDtypeStruct(r.shape, BF16),
        scratch_shapes=[pltpu.VMEM((N_PAIRS, LANES, LANES), F32)],
        compiler_params=_params(2),
        name="rwkv_scan",
    )(r, k, v, lw, kk, bb, bonus, lnx_g, lnx_b)


def kernel(x, moba_norm_g, moba_w_in, moba_w_out, rwkv_norm_g, rwkv_mix, rwkv_w_in, rwkv_w0, rwkv_w1, rwkv_w2, rwkv_a0, rwkv_a1, rwkv_a2, rwkv_k_k, rwkv_k_a, rwkv_r_k, rwkv_lnx_g, rwkv_lnx_b, rwkv_w_out, final_norm_g):
    batch, seq, d = x.shape
    nb = seq // MOBA_BLOCK
    assert d == D_MODEL and seq % MOBA_BLOCK == 0
    assert nb <= 32, "the chosen-block mask of a query is one int32"
    assert nb % ATTN_TILES == 0 and nb % ATTN_BLOCKS_PER_GROUP == 0
    assert seq % PROJ_ROWS == 0 and seq % (CHUNK * SCAN_CHUNKS_PER_STEP) == 0
    x2d = x.reshape(batch * seq, d)
    row = lambda t: t.reshape(1, -1).astype(F32)

    q, k, v, sg = _moba_proj(x2d, row(moba_norm_g), moba_w_in.astype(BF16), batch, seq,
                             tm=PROJ_ROWS)
    slopes = jnp.asarray([2.0 ** (-8.0 * (i + 1) / N_HEADS) for i in range(N_HEADS)], F32)
    slope_hi = (slopes * LOG2E).astype(BF16).astype(F32)
    sfeat = jnp.stack([slope_hi, slopes * LOG2E - slope_hi], axis=1).reshape(-1)
    o = _moba_attn(sfeat, slopes, q, k, v, unroll=ATTN_BLOCKS_PER_GROUP, tiles=ATTN_TILES)
    x1 = _out_proj(o, sg, x2d, moba_w_out.astype(BF16), row(final_norm_g), seq, tm=PROJ_ROWS,
                   final_norm=False)

    weights = (row(rwkv_norm_g), rwkv_mix.astype(F32), rwkv_w_in.astype(BF16),
               row(rwkv_w0), rwkv_w1.T.astype(BF16), rwkv_w2.astype(BF16),
               row(rwkv_a0), rwkv_a1.T.astype(BF16), rwkv_a2.astype(BF16),
               row(rwkv_k_k), row(rwkv_k_a), row(rwkv_r_k))
    r, k2, v2, lw, kk, bb, bonus, sg2 = _rwkv_proj(x1, weights, batch, seq, tm=PROJ_ROWS)
    pair_vec = lambda t: t.reshape(N_PAIRS, 1, LANES).astype(F32)
    o2 = _rwkv_scan(r, k2, v2, lw, kk, bb, bonus, pair_vec(rwkv_lnx_g), pair_vec(rwkv_lnx_b),
                    chunks_per_step=SCAN_CHUNKS_PER_STEP)
    out = _out_proj(o2, sg2, x1, rwkv_w_out.astype(BF16), row(final_norm_g), seq, tm=PROJ_ROWS,
                    final_norm=True)
    return out.reshape(batch, seq, d)
```
